```python
import math
import jax
import jax.numpy as jnp
from jax import lax
import numpy as np

D_MODEL = 2048
BATCH = 4
SEQ = 2048
DEPTH = 4

MIX_WIDTH = D_MODEL
ATTN_HEAD_DIM = 128
ATTN_WIDTH = MIX_WIDTH // 2
ATTN_HEADS = ATTN_WIDTH // ATTN_HEAD_DIM
MOBA_BLOCK = 256
MOBA_TOPK = 3
MOBA_QCHUNK = 32
REL_BUCKETS = 32
REL_MAX_DIST = 128
SSM_WIDTH = MIX_WIDTH - ATTN_WIDTH
SSM_HEAD_DIM = 64
SSM_HEADS = SSM_WIDTH // SSM_HEAD_DIM
SSM_GROUPS = 2
SSM_HEADS_PER_GROUP = SSM_HEADS // SSM_GROUPS
SSM_STATE = 128
SSM_CONV = 4
SSM_CHUNK = 128
SSM_CONV_CH = SSM_WIDTH + 2 * SSM_GROUPS * SSM_STATE
IN_COLS = 3 * ATTN_WIDTH + SSM_WIDTH + SSM_CONV_CH + SSM_HEADS
FFN_HIDDEN = 256 * (-(-8 * D_MODEL // (3 * 256)))
FFN_CONV = 3
NORM_EPS = 1e-6
NEG = -1e30

kernel_name = 'hybrid_moba_ssd_convffn'


def rms_norm(x, w):
    xf = x.astype(jnp.float32)
    y = xf * lax.rsqrt(jnp.mean(xf * xf, axis=-1, keepdims=True) + NORM_EPS)
    return (y * w.astype(jnp.float32)).astype(x.dtype)


def causal_depthwise_conv(x, w, b):
    k = w.shape[0]
    y = lax.conv_general_dilated(x, w[:, None, :].astype(x.dtype), window_strides=(1,), padding=[(k - 1, 0)],
                                 dimension_numbers=('NWC', 'WIO', 'NWC'), feature_group_count=x.shape[-1])
    return y + b.astype(x.dtype)


def t5_bucket(rel):
    n = jnp.maximum(rel, 0)
    max_exact = REL_BUCKETS // 2
    nf = jnp.maximum(n, 1).astype(jnp.float32)
    large = max_exact + (jnp.log(nf / max_exact) / math.log(REL_MAX_DIST / max_exact)
                         * (REL_BUCKETS - max_exact)).astype(jnp.int32)
    large = jnp.minimum(large, REL_BUCKETS - 1)
    return jnp.where(n < max_exact, n, large)


def moba_attention(q, k, v, rel_bias):
    bsz, s, h, dh = q.shape
    nb = -(-s // MOBA_BLOCK)
    pad = nb * MOBA_BLOCK - s
    n_sel = min(MOBA_TOPK, nb - 1)
    scale = dh ** -0.5
    qh = q.transpose(0, 2, 1, 3)
    kh = jnp.pad(k.transpose(0, 2, 1, 3), ((0, 0), (0, 0), (0, pad), (0, 0)))
    vh = jnp.pad(v.transpose(0, 2, 1, 3), ((0, 0), (0, 0), (0, pad), (0, 0)))
    kb = kh.reshape(bsz, h, nb, MOBA_BLOCK, dh)
    vb = vh.reshape(bsz, h, nb, MOBA_BLOCK, dh)
    bias_h = rel_bias.T
    nqc = s // MOBA_QCHUNK
    q_chunks = qh.reshape(bsz, h, nqc, MOBA_QCHUNK, dh).transpose(2, 0, 1, 3, 4)
    t_blk = jnp.arange(MOBA_BLOCK)
    b_idx = jnp.arange(bsz)[:, None, None, None]
    h_idx = jnp.arange(h)[None, :, None, None]
    kmean = kb.mean(axis=3) if n_sel > 0 else None

    def one_chunk(args):
        c, qc = args
        q0 = c * MOBA_QCHUNK
        qpos = q0 + jnp.arange(MOBA_QCHUNK)
        own = q0 // MOBA_BLOCK
        k_own = lax.dynamic_slice_in_dim(kh, own * MOBA_BLOCK, MOBA_BLOCK, axis=2)
        v_own = lax.dynamic_slice_in_dim(vh, own * MOBA_BLOCK, MOBA_BLOCK, axis=2)
        rel_own = qpos[:, None] - (own * MOBA_BLOCK + t_blk)[None, :]
        lg_own = (jnp.einsum('bhqd,bhtd->bhqt', qc, k_own) * scale).astype(jnp.float32) \
            + bias_h[:, t5_bucket(rel_own)].astype(jnp.float32)
        lg_own = jnp.where(rel_own >= 0, lg_own, NEG)
        if n_sel == 0:
            p_own = jax.nn.softmax(lg_own, axis=-1).astype(v.dtype)
            return jnp.einsum('bhqt,bhtd->bhqd', p_own, v_own)
        gate = jnp.einsum('bhqd,bhnd->bhqn', qc, kmean).astype(jnp.float32)
        gate = jnp.where(jnp.arange(nb) < own, gate, NEG)
        _, idx = lax.top_k(gate, n_sel)
        valid = idx < own
        k_sel = kb[b_idx, h_idx, idx]
        v_sel = vb[b_idx, h_idx, idx]
        rel_sel = qpos[None, None, :, None, None] - (idx[..., None] * MOBA_BLOCK + t_blk)
        lg_sel = (jnp.einsum('bhqd,bhqjtd->bhqjt', qc, k_sel) * scale).astype(jnp.float32) \
            + bias_h[h_idx[..., None], t5_bucket(rel_sel)].astype(jnp.float32)
        lg_sel = jnp.where(valid[..., None], lg_sel, NEG)
        n_g = n_sel * MOBA_BLOCK
        lg = jnp.concatenate([lg_sel.reshape(bsz, h, MOBA_QCHUNK, n_g), lg_own], axis=-1)
        p = jax.nn.softmax(lg, axis=-1).astype(v.dtype)
        p_sel = p[..., :n_g].reshape(bsz, h, MOBA_QCHUNK, n_sel, MOBA_BLOCK)
        return jnp.einsum('bhqjt,bhqjtd->bhqd', p_sel, v_sel) + jnp.einsum('bhqt,bhtd->bhqd', p[..., n_g:], v_own)

    out = lax.map(one_chunk, (jnp.arange(nqc), q_chunks))
    return out.transpose(1, 0, 3, 2, 4).reshape(bsz, s, h * dh)


def ssd_chunked(x, dt, a, bm, cm):
    bsz, s, g, hg, p = x.shape
    n = bm.shape[-1]
    L = SSM_CHUNK
    nc = s // L
    xc = (x * dt[..., None]).reshape(bsz, nc, L, g, hg, p)
    a_cs = jnp.cumsum((dt * a).reshape(bsz, nc, L, g, hg), axis=2)
    bc = bm.reshape(bsz, nc, L, g, n)
    cc = cm.reshape(bsz, nc, L, g, n)
    seg = a_cs[:, :, :, None] - a_cs[:, :, None, :]
    causal = jnp.tril(jnp.ones((L, L), dtype=bool))[:, :, None, None]
    decay = jnp.exp(jnp.where(causal, seg, -jnp.inf))
    cb = jnp.einsum('bclgn,bcsgn->bclsg', cc, bc)
    y_diag = jnp.einsum('bclsg,bclsgh,bcsghp->bclghp', cb, decay, xc)
    decay_states = jnp.exp(a_cs[:, :, -1:] - a_cs)
    states = jnp.einsum('bclgn,bclgh,bclghp->bcghpn', bc, decay_states, xc)
    chunk_decay = jnp.exp(a_cs[:, :, -1])

    def step(hstate, inp):
        st, dec = inp
        return hstate * dec[..., None, None] + st, hstate

    _, prev = lax.scan(step, jnp.zeros_like(states[:, 0]),
                       (jnp.moveaxis(states, 1, 0), jnp.moveaxis(chunk_decay, 1, 0)))
    prev = jnp.moveaxis(prev, 0, 1)
    y_off = jnp.einsum('bclgn,bcghpn,bclgh->bclghp', cc, prev, jnp.exp(a_cs))
    return (y_diag + y_off).reshape(bsz, s, g, hg, p)


def mamba2_mixer(z, xbc, dt_raw, conv_w, conv_b, dt_bias, a_log, d_skip, norm_w):
    bsz, s, _ = z.shape
    f32 = jnp.float32
    xbc = jax.nn.silu(causal_depthwise_conv(xbc, conv_w, conv_b))
    xs, bm, cm = jnp.split(xbc, [SSM_WIDTH, SSM_WIDTH + SSM_GROUPS * SSM_STATE], axis=-1)
    xs = xs.reshape(bsz, s, SSM_GROUPS, SSM_HEADS_PER_GROUP, SSM_HEAD_DIM).astype(f32)
    bm = bm.reshape(bsz, s, SSM_GROUPS, SSM_STATE).astype(f32)
    cm = cm.reshape(bsz, s, SSM_GROUPS, SSM_STATE).astype(f32)
    dt = jax.nn.softplus(dt_raw.astype(f32) + dt_bias.astype(f32)).reshape(bsz, s, SSM_GROUPS, SSM_HEADS_PER_GROUP)
    a = -jnp.exp(a_log.astype(f32)).reshape(SSM_GROUPS, SSM_HEADS_PER_GROUP)
    y = ssd_chunked(xs, dt, a, bm, cm) + d_skip.astype(f32).reshape(SSM_GROUPS, SSM_HEADS_PER_GROUP, 1) * xs
    gw = SSM_WIDTH // SSM_GROUPS
    y = y.reshape(bsz, s, SSM_GROUPS, gw) * jax.nn.silu(z.astype(f32)).reshape(bsz, s, SSM_GROUPS, gw)
    y = y * lax.rsqrt(jnp.mean(y * y, axis=-1, keepdims=True) + NORM_EPS)
    return (y.reshape(bsz, s, SSM_WIDTH) * norm_w.astype(f32)).astype(z.dtype)


def setup_inputs(seed: int = 0) -> dict:
    key = jax.random.key(seed)
    ks = jax.random.split(key, 18)
    f32 = jnp.float32

    def normal(k, shape, scale):
        return jax.random.normal(k, shape, f32) * scale

    def gain(k, width):
        return 1.0 + normal(k, (DEPTH, width), 0.02)

    x = normal(ks[0], (BATCH, SEQ, D_MODEL), 1.0)
    rel_bias = normal(ks[1], (REL_BUCKETS, ATTN_HEADS), 0.5)
    ln_mix_pre = gain(ks[2], D_MODEL)
    w_in = normal(ks[3], (DEPTH, D_MODEL, IN_COLS), D_MODEL ** -0.5)
    ssm_conv_w = normal(ks[4], (DEPTH, SSM_CONV, SSM_CONV_CH), SSM_CONV ** -0.5)
    ssm_conv_b = normal(ks[5], (DEPTH, SSM_CONV_CH), 0.02)
    dt0 = jnp.exp(jax.random.uniform(ks[6], (DEPTH, SSM_HEADS), f32, math.log(1e-3), math.log(1e-1)))
    dt_bias = dt0 + jnp.log(-jnp.expm1(-dt0))
    a_log = jnp.log(jax.random.uniform(ks[7], (DEPTH, SSM_HEADS), f32, 1.0, 16.0))
    d_skip = 1.0 + normal(ks[8], (DEPTH, SSM_HEADS), 0.1)
    ssm_norm_w = gain(ks[9], SSM_WIDTH)
    w_out = normal(ks[10], (DEPTH, MIX_WIDTH, D_MODEL), MIX_WIDTH ** -0.5)
    ln_mix_post = gain(ks[11], D_MODEL)
    ln_ffn_pre = gain(ks[12], D_MODEL)
    w_ffn_up = normal(ks[13], (DEPTH, D_MODEL, 2 * FFN_HIDDEN), D_MODEL ** -0.5)
    ffn_conv_w = normal(ks[14], (DEPTH, FFN_CONV, 2 * FFN_HIDDEN), FFN_CONV ** -0.5)
    ffn_conv_b = normal(ks[15], (DEPTH, 2 * FFN_HIDDEN), 0.02)
    w_ffn_down = normal(ks[16], (DEPTH, FFN_HIDDEN, D_MODEL), FFN_HIDDEN ** -0.5)
    ln_ffn_post = gain(ks[17], D_MODEL)
    return {'x': x, 'rel_bias': rel_bias, 'ln_mix_pre': ln_mix_pre, 'w_in': w_in,
            'ssm_conv_w': ssm_conv_w, 'ssm_conv_b': ssm_conv_b, 'dt_bias': dt_bias, 'a_log': a_log,
            'd_skip': d_skip, 'ssm_norm_w': ssm_norm_w, 'w_out': w_out, 'ln_mix_post': ln_mix_post,
            'ln_ffn_pre': ln_ffn_pre, 'w_ffn_up': w_ffn_up, 'ffn_conv_w': ffn_conv_w,
            'ffn_conv_b': ffn_conv_b, 'w_ffn_down': w_ffn_down, 'ln_ffn_post': ln_ffn_post}


def reference(x, rel_bias, ln_mix_pre, w_in, ssm_conv_w, ssm_conv_b, dt_bias, a_log, d_skip,
              ssm_norm_w, w_out, ln_mix_post, ln_ffn_pre, w_ffn_up, ffn_conv_w, ffn_conv_b,
              w_ffn_down, ln_ffn_post):
    bsz, s, _ = x.shape
    aw, sw = ATTN_WIDTH, SSM_WIDTH
    splits = [aw, 2 * aw, 3 * aw, 3 * aw + sw, 3 * aw + sw + SSM_CONV_CH]
    for l in range(DEPTH):
        h = rms_norm(x, ln_mix_pre[l])
        proj = h @ w_in[l]
        q, k, v, z, xbc, dt_raw = jnp.split(proj, splits, axis=-1)
        shp = (bsz, s, ATTN_HEADS, ATTN_HEAD_DIM)
        attn = moba_attention(q.reshape(shp), k.reshape(shp), v.reshape(shp), rel_bias)
        ssm = mamba2_mixer(z, xbc, dt_raw, ssm_conv_w[l], ssm_conv_b[l], dt_bias[l], a_log[l],
                           d_skip[l], ssm_norm_w[l])
        mixed = jnp.concatenate([attn, ssm], axis=-1) @ w_out[l]
        x = x + rms_norm(mixed, ln_mix_post[l])
        h = rms_norm(x, ln_ffn_pre[l])
        u = causal_depthwise_conv(h @ w_ffn_up[l], ffn_conv_w[l], ffn_conv_b[l])
        gate, up = jnp.split(u, [FFN_HIDDEN], axis=-1)
        f = (jax.nn.gelu(gate, approximate=True) * up) @ w_ffn_down[l]
        x = x + rms_norm(f, ln_ffn_post[l])
    return x
```

```python
import functools
import math

import jax
import jax.numpy as jnp
import numpy as np
from jax import lax
from jax.experimental import pallas as pl
from jax.experimental.pallas import tpu as pltpu

F32 = jnp.float32
BF16 = jnp.bfloat16

LANES = 128
SUBLANES = 8
VMEM_LIMIT_BYTES = 56 * 1024 * 1024

D_MODEL = 2048
ATTN_HEAD_DIM = 128
ATTN_WIDTH = 1024
ATTN_HEADS = ATTN_WIDTH // ATTN_HEAD_DIM
MOBA_BLOCK = 256
MOBA_TOPK = 3
REL_BUCKETS = 32
REL_MAX_DIST = 128
SSM_WIDTH = 1024
SSM_HEAD_DIM = 64
SSM_HEADS = SSM_WIDTH // SSM_HEAD_DIM
SSM_GROUPS = 2
SSM_GROUP_WIDTH = SSM_WIDTH // SSM_GROUPS
SSM_STATE = 128
SSM_CONV = 4
SSM_CHUNK = 128
SSM_BC_WIDTH = 2 * SSM_GROUPS * SSM_STATE
FFN_HIDDEN = 5632
FFN_CONV = 3
NORM_EPS = 1e-6
NEG = -1e30

QKV_COLS = 3 * ATTN_WIDTH
REST_COLS = SSM_WIDTH + SSM_WIDTH + SSM_BC_WIDTH + LANES
CONV_HALO = SUBLANES


def _params(*semantics):
    return pltpu.CompilerParams(dimension_semantics=semantics, vmem_limit_bytes=VMEM_LIMIT_BYTES)


def _rms(x, w):
    return x * lax.rsqrt(jnp.mean(x * x, axis=-1, keepdims=True) + NORM_EPS) * w


def _split3(x):
    hi = x.astype(BF16)
    r1 = x - hi.astype(F32)
    mid = r1.astype(BF16)
    lo = (r1 - mid.astype(F32)).astype(BF16)
    return hi, mid, lo


def _dot(a, b):
    return jnp.dot(a, b, preferred_element_type=F32)


def _dot_nt(a, b):
    return lax.dot_general(a, b, (((1,), (1,)), ((), ())), preferred_element_type=F32)


def _rmsnorm_kernel(x_ref, w_ref, o_ref):
    o_ref[...] = _rms(x_ref[...], w_ref[...]).astype(o_ref.dtype)


def _rmsnorm(x, w, tm=512):
    t, d = x.shape
    return pl.pallas_call(
        _rmsnorm_kernel,
        grid=(t // tm,),
        in_specs=[pl.BlockSpec((tm, d), lambda m: (m, 0)),
                  pl.BlockSpec((1, d), lambda m: (0, 0))],
        out_specs=pl.BlockSpec((tm, d), lambda m: (m, 0)),
        out_shape=jax.ShapeDtypeStruct((t, d), BF16),
        compiler_params=_params("parallel"),
        name="rmsnorm",
    )(x, w.reshape(1, d))


def _matmul_kernel(a_ref, w_ref, o_ref):
    o_ref[...] = _dot(a_ref[...], w_ref[...]).astype(o_ref.dtype)


def _matmul(a, w, out_dtype, tm, tn, name):
    t, k = a.shape
    n = w.shape[1]
    return pl.pallas_call(
        _matmul_kernel,
        grid=(n // tn, t // tm),
        in_specs=[pl.BlockSpec((tm, k), lambda j, m: (m, 0)),
                  pl.BlockSpec((k, tn), lambda j, m: (0, j))],
        out_specs=pl.BlockSpec((tm, tn), lambda j, m: (m, j)),
        out_shape=jax.ShapeDtypeStruct((t, n), out_dtype),
        compiler_params=_params("parallel", "parallel"),
        name=name,
    )(a, w)


def _t5_bucket_np(rel):
    n = np.maximum(rel, 0)
    max_exact = REL_BUCKETS // 2
    nf = np.maximum(n, 1).astype(np.float32)
    large = max_exact + (np.log(nf / np.float32(max_exact)) / np.float32(math.log(REL_MAX_DIST / max_exact))
                         * np.float32(REL_BUCKETS - max_exact)).astype(np.int32)
    large = np.minimum(large, REL_BUCKETS - 1)
    return np.where(n < max_exact, n, large).astype(np.int32)


def _bias_tiles_kernel(rb_ref, bown_ref, bprev_ref, town_ref, tprev_ref):
    h = pl.program_id(0)
    bown = bown_ref[...]
    bprev = bprev_ref[...]
    town = jnp.full(bown.shape, NEG, F32)
    tprev = jnp.zeros(bprev.shape, F32)
    for b in range(REL_BUCKETS):
        val = rb_ref[h, b]
        town = jnp.where(bown == b, val, town)
        tprev = jnp.where(bprev == b, val, tprev)
    town_ref[...] = town
    tprev_ref[...] = tprev


def _bias_tiles(rel_bias_t):
    r = np.arange(MOBA_BLOCK)
    rel_own = r[:, None] - r[None, :]
    bown = np.where(rel_own >= 0, _t5_bucket_np(rel_own), -1).astype(np.int32)
    bprev = _t5_bucket_np(MOBA_BLOCK + rel_own)
    tile = pl.BlockSpec((MOBA_BLOCK, MOBA_BLOCK), lambda h: (0, 0))
    out_tile = pl.BlockSpec((None, MOBA_BLOCK, MOBA_BLOCK), lambda h: (h, 0, 0))
    shape = jax.ShapeDtypeStruct((ATTN_HEADS, MOBA_BLOCK, MOBA_BLOCK), F32)
    return pl.pallas_call(
        _bias_tiles_kernel,
        grid=(ATTN_HEADS,),
        in_specs=[pl.BlockSpec(memory_space=pltpu.SMEM), tile, tile],
        out_specs=[out_tile, out_tile],
        out_shape=[shape, shape],
        compiler_params=_params("parallel"),
        name="t5_bias_tiles",
    )(rel_bias_t, jnp.asarray(bown), jnp.asarray(bprev))


def _moba_kernel(rb_ref, q_ref, k_ref, v_ref, town_ref, tprev_ref, o_ref, kaug_ref, km_ref, kmhi_ref, kmlo_ref):
    h = pl.program_id(1)
    i = pl.program_id(2)
    nb = k_ref.shape[0] // MOBA_BLOCK
    scale = ATTN_HEAD_DIM ** -0.5

    @pl.when(i == 0)
    def _():
        kaug_ref[:, 0:ATTN_HEAD_DIM] = k_ref[...]
        lane = lax.broadcasted_iota(jnp.int32, (MOBA_BLOCK, LANES), 1)
        km_ref[...] = jnp.zeros(km_ref.shape, F32)
        for jb in range(nb):
            rows = pl.ds(jb * MOBA_BLOCK, MOBA_BLOCK)
            kaug_ref[rows, ATTN_HEAD_DIM:] = jnp.where(lane == jb, 1.0, 0.0).astype(BF16)
            km_ref[jb:jb + 1, :] = jnp.mean(k_ref[rows, :].astype(F32), axis=0, keepdims=True)
        km = km_ref[...]
        hi = km.astype(BF16)
        kmhi_ref[...] = hi
        kmlo_ref[...] = (km - hi.astype(F32)).astype(BF16)

    q = q_ref[...]

    gate_t = (_dot_nt(kmhi_ref[...], q) + _dot_nt(kmlo_ref[...], q))[0:SUBLANES, :]
    blk = lax.broadcasted_iota(jnp.int32, gate_t.shape, 0)
    own = jnp.full(gate_t.shape, i, jnp.int32)
    rank = jnp.zeros(gate_t.shape, jnp.int32)
    for jp in range(nb - 1):
        row = gate_t[jp:jp + 1, :]
        beats = (row > gate_t) | ((row == gate_t) & (jp < blk))
        rank = rank + jnp.where(beats & (jp < own), 1, 0)
    sel_t = (rank < MOBA_TOPK) & (blk < own)
    mneg_t = jnp.where(sel_t, 0.0, NEG)
    mneg_t = jnp.concatenate([mneg_t, jnp.zeros((LANES - SUBLANES, MOBA_BLOCK), F32)], axis=0)
    q_aug = jnp.concatenate([q, mneg_t.T.astype(BF16)], axis=1)

    def attend(carry, s, rows):
        m, l, acc = carry
        m_new = jnp.maximum(m, jnp.max(s, axis=1, keepdims=True))
        alpha = jnp.exp(m - m_new)
        p = jnp.exp(s - m_new)
        l = alpha * l + jnp.sum(p, axis=1, keepdims=True)
        acc = alpha * acc + _dot(p.astype(BF16), v_ref[rows, :])
        return m_new, l, acc

    own_rows = pl.ds(pl.multiple_of(i * MOBA_BLOCK, MOBA_BLOCK), MOBA_BLOCK)
    s = _dot_nt(q, kaug_ref[own_rows, 0:ATTN_HEAD_DIM]) * scale + town_ref[...]
    m0 = jnp.max(s, axis=1, keepdims=True)
    p0 = jnp.exp(s - m0)
    carry = (m0, jnp.sum(p0, axis=1, keepdims=True), _dot(p0.astype(BF16), v_ref[own_rows, :]))

    jprev = jnp.maximum(i - 1, 0)
    prev_rows = pl.ds(pl.multiple_of(jprev * MOBA_BLOCK, MOBA_BLOCK), MOBA_BLOCK)
    s = _dot_nt(q_aug, kaug_ref[prev_rows, :]) * scale + tprev_ref[...]
    carry = attend(carry, s, prev_rows)

    far_bias = rb_ref[h, REL_BUCKETS - 1]

    def far_body(j, carry):
        rows = pl.ds(pl.multiple_of(j * MOBA_BLOCK, MOBA_BLOCK), MOBA_BLOCK)
        s = _dot_nt(q_aug, kaug_ref[rows, :]) * scale + far_bias
        return attend(carry, s, rows)

    _, l, acc = lax.fori_loop(0, jprev, far_body, carry)
    o_ref[...] = (acc / l).astype(o_ref.dtype)


def _moba(qkv, rel_bias_t, town, tprev, bsz, seq):
    nq = seq // MOBA_BLOCK
    assert (MOBA_BLOCK + MOBA_BLOCK) > REL_MAX_DIST
    tile = pl.BlockSpec((None, MOBA_BLOCK, MOBA_BLOCK), lambda b, h, i: (h, 0, 0))
    return pl.pallas_call(
        _moba_kernel,
        grid=(bsz, ATTN_HEADS, nq),
        in_specs=[pl.BlockSpec(memory_space=pltpu.SMEM),
                  pl.BlockSpec((None, MOBA_BLOCK, ATTN_HEAD_DIM), lambda b, h, i: (b, i, h)),
                  pl.BlockSpec((None, seq, ATTN_HEAD_DIM), lambda b, h, i: (b, 0, ATTN_HEADS + h)),
                  pl.BlockSpec((None, seq, ATTN_HEAD_DIM), lambda b, h, i: (b, 0, 2 * ATTN_HEADS + h)),
                  tile, tile],
        out_specs=pl.BlockSpec((None, MOBA_BLOCK, ATTN_HEAD_DIM), lambda b, h, i: (b, i, h)),
        out_shape=jax.ShapeDtypeStruct((bsz, seq, ATTN_WIDTH), BF16),
        scratch_shapes=[pltpu.VMEM((seq, 2 * ATTN_HEAD_DIM), BF16),
                        pltpu.VMEM((LANES, ATTN_HEAD_DIM), F32),
                        pltpu.VMEM((LANES, ATTN_HEAD_DIM), BF16),
                        pltpu.VMEM((LANES, ATTN_HEAD_DIM), BF16)],
        compiler_params=_params("parallel", "parallel", "arbitrary"),
        name="moba_attention",
    )(rel_bias_t, qkv, qkv, qkv, town, tprev)


def _silu(x):
    return x * jax.nn.sigmoid(x)


def _ssd_kernel(z_ref, xs_ref, bc_ref, dt_ref, cwx_ref, cwbc_ref, cbx_ref, cbbc_ref, dtb_ref, alog_ref,
                dskip_ref, normw_ref, expand_ref, ltri_ref, o_ref, xbuf_ref, bcbuf_ref, st_ref, y_ref):
    c = pl.program_id(1)
    L = SSM_CHUNK
    H = CONV_HALO

    @pl.when(c == 0)
    def _():
        xbuf_ref[0:H, :] = jnp.zeros((H, xbuf_ref.shape[1]), F32)
        bcbuf_ref[0:H, :] = jnp.zeros((H, bcbuf_ref.shape[1]), F32)
        st_ref[...] = jnp.zeros(st_ref.shape, F32)

    xbuf_ref[H:H + L, :] = xs_ref[...]
    bcbuf_ref[H:H + L, :] = bc_ref[...]

    def conv_silu(buf_ref, w_ref, b_ref):
        acc = b_ref[...]
        for k in range(SSM_CONV):
            start = H - (SSM_CONV - 1) + k
            acc = acc + w_ref[k:k + 1, :] * buf_ref[start:start + L, :]
        return _silu(acc)

    xs = conv_silu(xbuf_ref, cwx_ref, cbx_ref)
    bc = conv_silu(bcbuf_ref, cwbc_ref, cbbc_ref)
    xbuf_ref[0:H, :] = xbuf_ref[L:L + H, :]
    bcbuf_ref[0:H, :] = bcbuf_ref[L:L + H, :]

    x_dt = dt_ref[...] + dtb_ref[...]
    dt = jnp.maximum(x_dt, 0.0) + jnp.log1p(jnp.exp(-jnp.abs(x_dt)))
    da = dt * (-jnp.exp(alog_ref[...]))
    ltri = ltri_ref[...]
    a_cs = sum(_dot(ltri, part) for part in _split3(da))
    a_last = a_cs[L - 1:L, :]
    decay_st = jnp.exp(a_last - a_cs)
    exp_acs = jnp.exp(a_cs)
    chunk_decay = jnp.broadcast_to(jnp.exp(a_last), (2 * SUBLANES, LANES))

    stacked = jnp.concatenate([dt, decay_st, exp_acs, chunk_decay], axis=0)
    expand = expand_ref[...]
    wide = sum(_dot(part, expand) for part in _split3(stacked))
    dt_w = wide[0:L]
    decay_st_w = wide[L:2 * L]
    exp_acs_w = wide[2 * L:3 * L]
    chunk_decay_w = wide[3 * L:3 * L + 1]

    xc = xs * dt_w
    a_cs_t = a_cs.T
    row_i = lax.broadcasted_iota(jnp.int32, (L, L), 0)
    col_i = lax.broadcasted_iota(jnp.int32, (L, L), 1)
    causal = row_i >= col_i
    first_head = col_i < SSM_HEAD_DIM
    heads_per_group = SSM_HEADS // SSM_GROUPS

    for g in range(SSM_GROUPS):
        gsl = slice(g * SSM_GROUP_WIDTH, (g + 1) * SSM_GROUP_WIDTH)
        bg = bc[:, g * SSM_STATE:(g + 1) * SSM_STATE]
        cg = bc[:, (SSM_GROUPS + g) * SSM_STATE:(SSM_GROUPS + g + 1) * SSM_STATE].astype(BF16)
        cb = _dot_nt(cg, bg.astype(BF16))
        prev = st_ref[:, gsl]
        y_off = _dot(cg, prev.astype(BF16)) * exp_acs_w[:, gsl]
        st_ref[:, gsl] = prev * chunk_decay_w[:, gsl] + _dot(bg.T.astype(BF16),
                                                             (xc[:, gsl] * decay_st_w[:, gsl]).astype(BF16))
        for pr in range(heads_per_group // 2):
            h0 = g * heads_per_group + 2 * pr
            psl = slice(h0 * SSM_HEAD_DIM, (h0 + 2) * SSM_HEAD_DIM)
            xcp = xc[:, psl].astype(BF16)
            ys = []
            for hh in (h0, h0 + 1):
                seg = a_cs[:, hh:hh + 1] - a_cs_t[hh:hh + 1, :]
                decay = jnp.exp(jnp.where(causal, seg, NEG))
                ys.append(_dot((cb * decay).astype(BF16), xcp))
            y_ref[:, psl] = jnp.where(first_head, ys[0], ys[1]) + y_off[:, 2 * pr * SSM_HEAD_DIM:
                                                                         (2 * pr + 2) * SSM_HEAD_DIM]

    y = (y_ref[...] + dskip_ref[...] * xs) * _silu(z_ref[...])
    normw = normw_ref[...]
    for g in range(SSM_GROUPS):
        gsl = slice(g * SSM_GROUP_WIDTH, (g + 1) * SSM_GROUP_WIDTH)
        yg = y[:, gsl]
        yg = yg * lax.rsqrt(jnp.mean(yg * yg, axis=-1, keepdims=True) + NORM_EPS)
        o_ref[:, gsl] = (yg * normw[:, gsl]).astype(o_ref.dtype)


def _ssd(rest, conv_w, conv_b, dt_bias, a_log, d_skip, norm_w, bsz, seq):
    L = SSM_CHUNK
    nc = seq // L
    pad_heads = LANES - SSM_HEADS
    expand = np.zeros((LANES, SSM_WIDTH), np.float32)
    for hh in range(SSM_HEADS):
        expand[hh, hh * SSM_HEAD_DIM:(hh + 1) * SSM_HEAD_DIM] = 1.0
    ltri = np.tril(np.ones((L, L), np.float32))

    row = lambda v: v.reshape(1, -1).astype(F32)
    const = lambda shape: pl.BlockSpec(shape, lambda b, c: (0, 0))
    args = (rest, rest, rest, rest,
            conv_w[:, :SSM_WIDTH], conv_w[:, SSM_WIDTH:], row(conv_b[:SSM_WIDTH]), row(conv_b[SSM_WIDTH:]),
            row(jnp.pad(dt_bias, (0, pad_heads))), row(jnp.pad(a_log, (0, pad_heads))),
            row(jnp.repeat(d_skip, SSM_HEAD_DIM)), row(norm_w),
            jnp.asarray(expand, BF16), jnp.asarray(ltri, BF16))
    in_specs = [
        pl.BlockSpec((None, L, SSM_WIDTH), lambda b, c: (b, c, 0)),
        pl.BlockSpec((None, L, SSM_WIDTH), lambda b, c: (b, c, 1)),
        pl.BlockSpec((None, L, SSM_BC_WIDTH), lambda b, c: (b, c, 2 * SSM_WIDTH // SSM_BC_WIDTH)),
        pl.BlockSpec((None, L, LANES), lambda b, c: (b, c, (2 * SSM_WIDTH + SSM_BC_WIDTH) // LANES)),
        const((SSM_CONV, SSM_WIDTH)), const((SSM_CONV, SSM_BC_WIDTH)),
        const((1, SSM_WIDTH)), const((1, SSM_BC_WIDTH)),
        const((1, LANES)), const((1, LANES)), const((1, SSM_WIDTH)), const((1, SSM_WIDTH)),
        const((LANES, SSM_WIDTH)), const((L, L)),
    ]
    return pl.pallas_call(
        _ssd_kernel,
        grid=(bsz, nc),
        in_specs=in_specs,
        out_specs=pl.BlockSpec((None, L, SSM_WIDTH), lambda b, c: (b, c, 0)),
        out_shape=jax.ShapeDtypeStruct((bsz, seq, SSM_WIDTH), BF16),
        scratch_shapes=[pltpu.VMEM((L + CONV_HALO, SSM_WIDTH), F32),
                        pltpu.VMEM((L + CONV_HALO, SSM_BC_WIDTH), F32),
                        pltpu.VMEM((SSM_STATE, SSM_WIDTH), F32),
                        pltpu.VMEM((L, SSM_WIDTH), F32)],
        compiler_params=_params("parallel", "arbitrary"),
        name="ssd_mixer",
    )(*args)


def _outproj_kernel(attn_ref, ssm_ref, w_ref, x_ref, gpost_ref, gnext_ref, xo_ref, ho_ref):
    mixed = _dot(attn_ref[...], w_ref[0:ATTN_WIDTH, :]) + _dot(ssm_ref[...], w_ref[ATTN_WIDTH:, :])
    x_new = x_ref[...] + _rms(mixed, gpost_ref[...])
    xo_ref[...] = x_new
    ho_ref[...] = _rms(x_new, gnext_ref[...]).astype(ho_ref.dtype)


def _outproj(attn, ssm, w, x, g_post, g_next, tm=512):
    t, d = x.shape
    row = pl.BlockSpec((1, d), lambda m: (0, 0))
    return pl.pallas_call(
        _outproj_kernel,
        grid=(t // tm,),
        in_specs=[pl.BlockSpec((tm, ATTN_WIDTH), lambda m: (m, 0)),
                  pl.BlockSpec((tm, SSM_WIDTH), lambda m: (m, 0)),
                  pl.BlockSpec(w.shape, lambda m: (0, 0)),
                  pl.BlockSpec((tm, d), lambda m: (m, 0)),
                  row, row],
        out_specs=[pl.BlockSpec((tm, d), lambda m: (m, 0)), pl.BlockSpec((tm, d), lambda m: (m, 0))],
        out_shape=[jax.ShapeDtypeStruct((t, d), F32), jax.ShapeDtypeStruct((t, d), BF16)],
        compiler_params=_params("parallel"),
        name="out_proj",
    )(attn, ssm, w, x, g_post.reshape(1, d), g_next.reshape(1, d))


def _ffn_up_kernel(h_ref, wg_ref, wu_ref, cwg_ref, cwu_ref, cbg_ref, cbu_ref, o_ref, gbuf_ref, ubuf_ref,
                   *, tiles_per_seq):
    m = pl.program_id(1)
    tm = h_ref.shape[0]
    H = CONV_HALO

    @pl.when(m % tiles_per_seq == 0)
    def _():
        gbuf_ref[0:H, :] = jnp.zeros((H, gbuf_ref.shape[1]), F32)
        ubuf_ref[0:H, :] = jnp.zeros((H, ubuf_ref.shape[1]), F32)

    hmat = h_ref[...]

    def conv(buf_ref, w_ref, cw_ref, cb_ref):
        buf_ref[H:H + tm, :] = _dot(hmat, w_ref[...])
        acc = cb_ref[...]
        for k in range(FFN_CONV):
            start = H - (FFN_CONV - 1) + k
            acc = acc + cw_ref[k:k + 1, :] * buf_ref[start:start + tm, :]
        buf_ref[0:H, :] = buf_ref[tm:tm + H, :]
        return acc

    gate = conv(gbuf_ref, wg_ref, cwg_ref, cbg_ref)
    up = conv(ubuf_ref, wu_ref, cwu_ref, cbu_ref)
    o_ref[...] = (jax.nn.gelu(gate, approximate=True) * up).astype(o_ref.dtype)


def _ffn_up(h, w_up, conv_w, conv_b, seq, tm=512, tn=512):
    t, d = h.shape
    nt = FFN_HIDDEN // tn
    conv_b = conv_b.reshape(1, -1)
    kernel = functools.partial(_ffn_up_kernel, tiles_per_seq=seq // tm)
    return pl.pallas_call(
        kernel,
        grid=(nt, t // tm),
        in_specs=[pl.BlockSpec((tm, d), lambda j, m: (m, 0)),
                  pl.BlockSpec((d, tn), lambda j, m: (0, j)),
                  pl.BlockSpec((d, tn), lambda j, m: (0, nt + j)),
                  pl.BlockSpec((FFN_CONV, tn), lambda j, m: (0, j)),
                  pl.BlockSpec((FFN_CONV, tn), lambda j, m: (0, nt + j)),
                  pl.BlockSpec((1, tn), lambda j, m: (0, j)),
                  pl.BlockSpec((1, tn), lambda j, m: (0, nt + j))],
        out_specs=pl.BlockSpec((tm, tn), lambda j, m: (m, j)),
        out_shape=jax.ShapeDtypeStruct((t, FFN_HIDDEN), BF16),
        scratch_shapes=[pltpu.VMEM((tm + CONV_HALO, tn), F32), pltpu.VMEM((tm + CONV_HALO, tn), F32)],
        compiler_params=_params("parallel", "arbitrary"),
        name="ffn_up_conv_geglu",
    )(h, w_up, w_up, conv_w, conv_w, conv_b, conv_b)


def _ffn_down_kernel(a_ref, w_ref, x_ref, gpost_ref, gnext_ref, xo_ref, ho_ref, acc_ref):
    k = pl.program_id(1)

    @pl.when(k == 0)
    def _():
        acc_ref[...] = jnp.zeros(acc_ref.shape, F32)

    acc_ref[...] += _dot(a_ref[...], w_ref[...])

    @pl.when(k == pl.num_programs(1) - 1)
    def _():
        x_new = x_ref[...] + _rms(acc_ref[...], gpost_ref[...])
        xo_ref[...] = x_new
        ho_ref[...] = _rms(x_new, gnext_ref[...]).astype(ho_ref.dtype)


def _ffn_down(act, w, x, g_post, g_next, tm=512, tk=1408):
    t, d = x.shape
    kdim = act.shape[1]
    row = pl.BlockSpec((1, d), lambda m, k: (0, 0))
    return pl.pallas_call(
        _ffn_down_kernel,
        grid=(t // tm, kdim // tk),
        in_specs=[pl.BlockSpec((tm, tk), lambda m, k: (m, k)),
                  pl.BlockSpec((tk, d), lambda m, k: (k, 0)),
                  pl.BlockSpec((tm, d), lambda m, k: (m, 0)),
                  row, row],
        out_specs=[pl.BlockSpec((tm, d), lambda m, k: (m, 0)), pl.BlockSpec((tm, d), lambda m, k: (m, 0))],
        out_shape=[jax.ShapeDtypeStruct((t, d), F32), jax.ShapeDtypeStruct((t, d), BF16)],
        scratch_shapes=[pltpu.VMEM((tm, d), F32)],
        compiler_params=_params("parallel", "arbitrary"),
        name="ffn_down",
    )(act, w, x, g_post.reshape(1, d), g_next.reshape(1, d))


def kernel(x, rel_bias, ln_mix_pre, w_in, ssm_conv_w, ssm_conv_b, dt_bias, a_log, d_skip, ssm_norm_w, w_out,
           ln_mix_post, ln_ffn_pre, w_ffn_up, ffn_conv_w, ffn_conv_b, w_ffn_down, ln_ffn_post):
    bsz, seq, d = x.shape
    depth = w_in.shape[0]
    t = bsz * seq
    in_cols = w_in.shape[2]

    rel_bias_t = rel_bias.T.astype(F32)
    town, tprev = _bias_tiles(rel_bias_t)

    xf = x.reshape(t, d)
    h = _rmsnorm(xf, ln_mix_pre[0])
    for l in range(depth):
        w_qkv = w_in[l, :, :QKV_COLS].astype(BF16)
        w_rest = jnp.pad(w_in[l, :, QKV_COLS:], ((0, 0), (0, QKV_COLS + REST_COLS - in_cols))).astype(BF16)
        qkv = _matmul(h, w_qkv, BF16, tm=512, tn=1024, name="in_proj_qkv")
        rest = _matmul(h, w_rest, F32, tm=512, tn=896, name="in_proj_ssm")
        attn = _moba(qkv.reshape(bsz, seq, QKV_COLS), rel_bias_t, town, tprev, bsz, seq)
        ssm = _ssd(rest.reshape(bsz, seq, REST_COLS), ssm_conv_w[l], ssm_conv_b[l], dt_bias[l], a_log[l],
                   d_skip[l], ssm_norm_w[l], bsz, seq)
        xf, h = _outproj(attn.reshape(t, ATTN_WIDTH), ssm.reshape(t, SSM_WIDTH), w_out[l].astype(BF16), xf,
                         ln_mix_post[l], ln_ffn_pre[l])
        act = _ffn_up(h, w_ffn_up[l].astype(BF16), ffn_conv_w[l], ffn_conv_b[l], seq)
        g_next = ln_mix_pre[(l + 1) % depth]
        xf, h = _ffn_down(act, w_ffn_down[l].astype(BF16), xf, ln_ffn_post[l], g_next)
    return xf.reshape(bsz, seq, d)
```

```python
import functools
import math

import jax
import jax.numpy as jnp
import numpy as np
from jax import lax
from jax.experimental import pallas as pl
from jax.experimental.pallas import tpu as pltpu

F32 = jnp.float32
BF16 = jnp.bfloat16

LANES = 128
SUBLANES = 8
VMEM_LIMIT_BYTES = 56 * 1024 * 1024

D_MODEL = 2048
ATTN_HEAD_DIM = 128
ATTN_WIDTH = 1024
ATTN_HEADS = ATTN_WIDTH // ATTN_HEAD_DIM
MOBA_BLOCK = 256
MOBA_TOPK = 3
REL_BUCKETS = 32
REL_MAX_DIST = 128
SSM_WIDTH = 1024
SSM_HEAD_DIM = 64
SSM_HEADS = SSM_WIDTH // SSM_HEAD_DIM
SSM_GROUPS = 2
SSM_GROUP_WIDTH = SSM_WIDTH // SSM_GROUPS
SSM_STATE = 128
SSM_CONV = 4
SSM_CHUNK = 128
SSM_BC_WIDTH = 2 * SSM_GROUPS * SSM_STATE
FFN_HIDDEN = 5632
FFN_CONV = 3
NORM_EPS = 1e-6
NEG = -1e30

QKV_COLS = 3 * ATTN_WIDTH
REST_COLS = SSM_WIDTH + SSM_WIDTH + SSM_BC_WIDTH
CONV_HALO = SUBLANES


def _params(*semantics):
    return pltpu.CompilerParams(dimension_semantics=semantics, vmem_limit_bytes=VMEM_LIMIT_BYTES)


def _rms(x, w):
    return x * lax.rsqrt(jnp.mean(x * x, axis=-1, keepdims=True) + NORM_EPS) * w


def _split3(x):
    hi = x.astype(BF16)
    r1 = x - hi.astype(F32)
    mid = r1.astype(BF16)
    lo = (r1 - mid.astype(F32)).astype(BF16)
    return hi, mid, lo


def _dot(a, b):
    return jnp.dot(a, b, preferred_element_type=F32)


def _dot_nt(a, b):
    return lax.dot_general(a, b, (((1,), (1,)), ((), ())), preferred_element_type=F32)


def _rmsnorm_kernel(x_ref, w_ref, o_ref):
    o_ref[...] = _rms(x_ref[...], w_ref[...]).astype(o_ref.dtype)


def _rmsnorm(x, w, tm=512):
    t, d = x.shape
    return pl.pallas_call(
        _rmsnorm_kernel,
        grid=(t // tm,),
        in_specs=[pl.BlockSpec((tm, d), lambda m: (m, 0)),
                  pl.BlockSpec((1, d), lambda m: (0, 0))],
        out_specs=pl.BlockSpec((tm, d), lambda m: (m, 0)),
        out_shape=jax.ShapeDtypeStruct((t, d), BF16),
        compiler_params=_params("parallel"),
        name="rmsnorm",
    )(x, w.reshape(1, d))


def _matmul_kernel(a_ref, w_ref, o_ref, wb_ref):
    @pl.when(pl.program_id(1) == 0)
    def _():
        wb_ref[...] = w_ref[...].astype(BF16)

    o_ref[...] = _dot(a_ref[...], wb_ref[...]).astype(o_ref.dtype)


def _matmul(a, w, layer, first_tile, n_tiles, out_dtype, tm, tn, name):
    t, k = a.shape
    return pl.pallas_call(
        _matmul_kernel,
        grid=(n_tiles, t // tm),
        in_specs=[pl.BlockSpec((tm, k), lambda j, m: (m, 0)),
                  pl.BlockSpec((None, k, tn), lambda j, m: (layer, 0, first_tile + j))],
        out_specs=pl.BlockSpec((tm, tn), lambda j, m: (m, j)),
        out_shape=jax.ShapeDtypeStruct((t, n_tiles * tn), out_dtype),
        scratch_shapes=[pltpu.VMEM((k, tn), BF16)],
        compiler_params=_params("parallel", "arbitrary"),
        name=name,
    )(a, w)


def _t5_bucket_np(rel):
    n = np.maximum(rel, 0)
    max_exact = REL_BUCKETS // 2
    nf = np.maximum(n, 1).astype(np.float32)
    large = max_exact + (np.log(nf / np.float32(max_exact)) / np.float32(math.log(REL_MAX_DIST / max_exact))
                         * np.float32(REL_BUCKETS - max_exact)).astype(np.int32)
    large = np.minimum(large, REL_BUCKETS - 1)
    return np.where(n < max_exact, n, large).astype(np.int32)


def _bias_tiles_kernel(rb_ref, bown_ref, bprev_ref, town_ref, tprev_ref):
    h = pl.program_id(0)
    bown = bown_ref[...]
    bprev = bprev_ref[...]
    town = jnp.full(bown.shape, NEG, F32)
    tprev = jnp.zeros(bprev.shape, F32)
    for b in range(REL_BUCKETS):
        val = rb_ref[h, b]
        town = jnp.where(bown == b, val, town)
        tprev = jnp.where(bprev == b, val, tprev)
    town_ref[...] = town
    tprev_ref[...] = tprev


def _bias_tiles(rel_bias_t):
    r = np.arange(MOBA_BLOCK)
    rel_own = r[None, :] - r[:, None]
    bown = np.where(rel_own >= 0, _t5_bucket_np(rel_own), -1).astype(np.int32)
    bprev = _t5_bucket_np(MOBA_BLOCK + rel_own)
    tile = pl.BlockSpec((MOBA_BLOCK, MOBA_BLOCK), lambda h: (0, 0))
    out_tile = pl.BlockSpec((None, MOBA_BLOCK, MOBA_BLOCK), lambda h: (h, 0, 0))
    shape = jax.ShapeDtypeStruct((ATTN_HEADS, MOBA_BLOCK, MOBA_BLOCK), F32)
    return pl.pallas_call(
        _bias_tiles_kernel,
        grid=(ATTN_HEADS,),
        in_specs=[pl.BlockSpec(memory_space=pltpu.SMEM), tile, tile],
        out_specs=[out_tile, out_tile],
        out_shape=[shape, shape],
        compiler_params=_params("parallel"),
        name="t5_bias_tiles",
    )(rel_bias_t, jnp.asarray(bown), jnp.asarray(bprev))


KM_ROWS = 2 * SUBLANES


def _moba_kernel(rb_ref, q_ref, k_ref, v_ref, town_ref, tprev_ref, o_ref, vt_ref, km_ref, s_ref):
    h = pl.program_id(1)
    nb = k_ref.shape[0] // MOBA_BLOCK
    scale = ATTN_HEAD_DIM ** -0.5
    far_bias = rb_ref[h, REL_BUCKETS - 1]
    blocks = [slice(jb * MOBA_BLOCK, (jb + 1) * MOBA_BLOCK) for jb in range(nb)]

    km_ref[...] = jnp.zeros(km_ref.shape, F32)
    for jb in range(nb):
        vt_ref[:, blocks[jb]] = v_ref[blocks[jb], :].astype(F32).T.astype(BF16)
        km_ref[jb:jb + 1, :] = jnp.mean(k_ref[blocks[jb], :].astype(F32), axis=0, keepdims=True)
    km = km_ref[...]
    km_hi = km.astype(BF16)
    km_lo = (km - km_hi.astype(F32)).astype(BF16)

    for i in range(nb):
        q = q_ref[blocks[i], :]
        gated = i > MOBA_TOPK
        if gated:
            gate_t = (_dot_nt(km_hi, q) + _dot_nt(km_lo, q))[0:SUBLANES, :]
            blk = lax.broadcasted_iota(jnp.int32, gate_t.shape, 0)
            rank = jnp.zeros(gate_t.shape, jnp.int32)
            for jp in range(i):
                row = gate_t[jp:jp + 1, :]
                beats = (row > gate_t) | ((row == gate_t) & (jp < blk))
                rank = rank + jnp.where(beats, 1, 0)
            mneg_t = jnp.where((rank < MOBA_TOPK) & (blk < i), 0.0, NEG)

        buf = i % 2
        m = None
        for j in range(i + 1):
            s = _dot_nt(k_ref[blocks[j], :], q) * scale
            if j == i:
                s = s + town_ref[...]
            else:
                mrow = mneg_t[j:j + 1, :] if gated else None
                if j == i - 1:
                    s = s + tprev_ref[...]
                    if gated:
                        s = s + mrow
                else:
                    s = s + (far_bias + mrow if gated else far_bias)
            s_ref[buf, j] = s
            cmax = jnp.max(s, axis=0, keepdims=True)
            m = cmax if m is None else jnp.maximum(m, cmax)

        l = jnp.zeros((1, MOBA_BLOCK), F32)
        acc = jnp.zeros((ATTN_HEAD_DIM, MOBA_BLOCK), F32)
        for j in range(i + 1):
            p = jnp.exp(s_ref[buf, j] - m)
            l = l + jnp.sum(p, axis=0, keepdims=True)
            acc = acc + _dot(vt_ref[:, blocks[j]], p.astype(BF16))
        o_ref[blocks[i], :] = (acc / l).T.astype(o_ref.dtype)


def _moba(qkv, rel_bias_t, town, tprev, bsz, seq):
    nb = seq // MOBA_BLOCK
    assert (MOBA_BLOCK + MOBA_BLOCK) > REL_MAX_DIST and nb <= SUBLANES
    tile = pl.BlockSpec((None, MOBA_BLOCK, MOBA_BLOCK), lambda b, h: (h, 0, 0))
    head_cols = lambda part: pl.BlockSpec((None, seq, ATTN_HEAD_DIM), lambda b, h: (b, 0, part * ATTN_HEADS + h))
    return pl.pallas_call(
        _moba_kernel,
        grid=(bsz, ATTN_HEADS),
        in_specs=[pl.BlockSpec(memory_space=pltpu.SMEM), head_cols(0), head_cols(1), head_cols(2), tile, tile],
        out_specs=pl.BlockSpec((None, seq, ATTN_HEAD_DIM), lambda b, h: (b, 0, h)),
        out_shape=jax.ShapeDtypeStruct((bsz, seq, ATTN_WIDTH), BF16),
        scratch_shapes=[pltpu.VMEM((ATTN_HEAD_DIM, seq), BF16),
                        pltpu.VMEM((KM_ROWS, ATTN_HEAD_DIM), F32),
                        pltpu.VMEM((2, nb, MOBA_BLOCK, MOBA_BLOCK), F32)],
        compiler_params=_params("parallel", "parallel"),
        name="moba_attention",
    )(rel_bias_t, qkv, qkv, qkv, town, tprev)


def _silu(x):
    return x * jax.nn.sigmoid(x)


def _ssd_kernel(z_ref, xs_ref, bc_ref, dt_ref, cwx_ref, cwbc_ref, cbx_ref, cbbc_ref, dtb_ref, alog_ref,
                dskip_ref, normw_ref, expand_ref, ltri_ref, o_ref, xbuf_ref, bcbuf_ref, st_ref, y_ref):
    c = pl.program_id(1)
    L = SSM_CHUNK
    H = CONV_HALO

    @pl.when(c == 0)
    def _():
        xbuf_ref[0:H, :] = jnp.zeros((H, xbuf_ref.shape[1]), F32)
        bcbuf_ref[0:H, :] = jnp.zeros((H, bcbuf_ref.shape[1]), F32)
        st_ref[...] = jnp.zeros(st_ref.shape, F32)

    xbuf_ref[H:H + L, :] = xs_ref[...]
    bcbuf_ref[H:H + L, :] = bc_ref[...]

    def conv_silu(buf_ref, w_ref, b_ref):
        acc = b_ref[...]
        for k in range(SSM_CONV):
            start = H - (SSM_CONV - 1) + k
            acc = acc + w_ref[k:k + 1, :] * buf_ref[start:start + L, :]
        return _silu(acc)

    xs = conv_silu(xbuf_ref, cwx_ref, cbx_ref)
    bc = conv_silu(bcbuf_ref, cwbc_ref, cbbc_ref)
    xbuf_ref[0:H, :] = xbuf_ref[L:L + H, :]
    bcbuf_ref[0:H, :] = bcbuf_ref[L:L + H, :]

    x_dt = dt_ref[...] + dtb_ref[...]
    dt = jnp.maximum(x_dt, 0.0) + jnp.log1p(jnp.exp(-jnp.abs(x_dt)))
    da = dt * (-jnp.exp(alog_ref[...]))
    ltri = ltri_ref[...]
    a_cs = sum(_dot(ltri, part) for part in _split3(da))
    a_last = a_cs[L - 1:L, :]
    decay_st = jnp.exp(a_last - a_cs)
    exp_acs = jnp.exp(a_cs)
    chunk_decay = jnp.broadcast_to(jnp.exp(a_last), (2 * SUBLANES, LANES))

    stacked = jnp.concatenate([dt, decay_st, exp_acs, chunk_decay], axis=0)
    expand = expand_ref[...]
    wide = sum(_dot(part, expand) for part in _split3(stacked))
    dt_w = wide[0:L]
    decay_st_w = wide[L:2 * L]
    exp_acs_w = wide[2 * L:3 * L]
    chunk_decay_w = wide[3 * L:3 * L + 1]

    xc = xs * dt_w
    a_cs_t = a_cs.T
    row_i = lax.broadcasted_iota(jnp.int32, (L, L), 0)
    col_i = lax.broadcasted_iota(jnp.int32, (L, L), 1)
    causal = row_i >= col_i
    first_head = col_i < SSM_HEAD_DIM
    heads_per_group = SSM_HEADS // SSM_GROUPS

    for g in range(SSM_GROUPS):
        gsl = slice(g * SSM_GROUP_WIDTH, (g + 1) * SSM_GROUP_WIDTH)
        bg = bc[:, g * SSM_STATE:(g + 1) * SSM_STATE]
        cg = bc[:, (SSM_GROUPS + g) * SSM_STATE:(SSM_GROUPS + g + 1) * SSM_STATE].astype(BF16)
        cb = _dot_nt(cg, bg.astype(BF16))
        prev = st_ref[:, gsl]
        y_off = _dot(cg, prev.astype(BF16)) * exp_acs_w[:, gsl]
        st_ref[:, gsl] = prev * chunk_decay_w[:, gsl] + _dot(bg.T.astype(BF16),
                                                             (xc[:, gsl] * decay_st_w[:, gsl]).astype(BF16))
        for pr in range(heads_per_group // 2):
            h0 = g * heads_per_group + 2 * pr
            psl = slice(h0 * SSM_HEAD_DIM, (h0 + 2) * SSM_HEAD_DIM)
            xcp = xc[:, psl].astype(BF16)
            ys = []
            for hh in (h0, h0 + 1):
                seg = a_cs[:, hh:hh + 1] - a_cs_t[hh:hh + 1, :]
                decay = jnp.exp(jnp.where(causal, seg, NEG))
                ys.append(_dot((cb * decay).astype(BF16), xcp))
            y_ref[:, psl] = jnp.where(first_head, ys[0], ys[1]) + y_off[:, 2 * pr * SSM_HEAD_DIM:
                                                                         (2 * pr + 2) * SSM_HEAD_DIM]

    y = (y_ref[...] + dskip_ref[...] * xs) * _silu(z_ref[...])
    normw = normw_ref[...]
    for g in range(SSM_GROUPS):
        gsl = slice(g * SSM_GROUP_WIDTH, (g + 1) * SSM_GROUP_WIDTH)
        yg = y[:, gsl]
        yg = yg * lax.rsqrt(jnp.mean(yg * yg, axis=-1, keepdims=True) + NORM_EPS)
        o_ref[:, gsl] = (yg * normw[:, gsl]).astype(o_ref.dtype)


def _ssd(rest, dt_raw, conv_w, conv_b, dt_bias, a_log, d_skip, norm_w, bsz, seq):
    L = SSM_CHUNK
    nc = seq // L
    pad_heads = LANES - SSM_HEADS
    expand = np.zeros((LANES, SSM_WIDTH), np.float32)
    for hh in range(SSM_HEADS):
        expand[hh, hh * SSM_HEAD_DIM:(hh + 1) * SSM_HEAD_DIM] = 1.0
    ltri = np.tril(np.ones((L, L), np.float32))

    row = lambda v: v.reshape(1, -1).astype(F32)
    const = lambda shape: pl.BlockSpec(shape, lambda b, c: (0, 0))
    args = (rest, rest, rest, dt_raw,
            conv_w[:, :SSM_WIDTH], conv_w[:, SSM_WIDTH:], row(conv_b[:SSM_WIDTH]), row(conv_b[SSM_WIDTH:]),
            row(jnp.pad(dt_bias, (0, pad_heads))), row(jnp.pad(a_log, (0, pad_heads))),
            row(jnp.repeat(d_skip, SSM_HEAD_DIM)), row(norm_w),
            jnp.asarray(expand, BF16), jnp.asarray(ltri, BF16))
    in_specs = [
        pl.BlockSpec((None, L, SSM_WIDTH), lambda b, c: (b, c, 0)),
        pl.BlockSpec((None, L, SSM_WIDTH), lambda b, c: (b, c, 1)),
        pl.BlockSpec((None, L, SSM_BC_WIDTH), lambda b, c: (b, c, 2 * SSM_WIDTH // SSM_BC_WIDTH)),
        pl.BlockSpec((None, L, LANES), lambda b, c: (b, c, 0)),
        const((SSM_CONV, SSM_WIDTH)), const((SSM_CONV, SSM_BC_WIDTH)),
        const((1, SSM_WIDTH)), const((1, SSM_BC_WIDTH)),
        const((1, LANES)), const((1, LANES)), const((1, SSM_WIDTH)), const((1, SSM_WIDTH)),
        const((LANES, SSM_WIDTH)), const((L, L)),
    ]
    return pl.pallas_call(
        _ssd_kernel,
        grid=(bsz, nc),
        in_specs=in_specs,
        out_specs=pl.BlockSpec((None, L, SSM_WIDTH), lambda b, c: (b, c, 0)),
        out_shape=jax.ShapeDtypeStruct((bsz, seq, SSM_WIDTH), BF16),
        scratch_shapes=[pltpu.VMEM((L + CONV_HALO, SSM_WIDTH), F32),
                        pltpu.VMEM((L + CONV_HALO, SSM_BC_WIDTH), F32),
                        pltpu.VMEM((SSM_STATE, SSM_WIDTH), F32),
                        pltpu.VMEM((L, SSM_WIDTH), F32)],
        compiler_params=_params("parallel", "arbitrary"),
        name="ssd_mixer",
    )(*args)


def _outproj_kernel(attn_ref, ssm_ref, w_ref, x_ref, gpost_ref, gnext_ref, xo_ref, ho_ref):
    mixed = _dot(attn_ref[...], w_ref[0:ATTN_WIDTH, :]) + _dot(ssm_ref[...], w_ref[ATTN_WIDTH:, :])
    x_new = x_ref[...] + _rms(mixed, gpost_ref[...])
    xo_ref[...] = x_new
    ho_ref[...] = _rms(x_new, gnext_ref[...]).astype(ho_ref.dtype)


def _outproj(attn, ssm, w, x, g_post, g_next, tm=512):
    t, d = x.shape
    row = pl.BlockSpec((1, d), lambda m: (0, 0))
    return pl.pallas_call(
        _outproj_kernel,
        grid=(t // tm,),
        in_specs=[pl.BlockSpec((tm, ATTN_WIDTH), lambda m: (m, 0)),
                  pl.BlockSpec((tm, SSM_WIDTH), lambda m: (m, 0)),
                  pl.BlockSpec(w.shape, lambda m: (0, 0)),
                  pl.BlockSpec((tm, d), lambda m: (m, 0)),
                  row, row],
        out_specs=[pl.BlockSpec((tm, d), lambda m: (m, 0)), pl.BlockSpec((tm, d), lambda m: (m, 0))],
        out_shape=[jax.ShapeDtypeStruct((t, d), F32), jax.ShapeDtypeStruct((t, d), BF16)],
        compiler_params=_params("parallel"),
        name="out_proj",
    )(attn, ssm, w, x, g_post.reshape(1, d), g_next.reshape(1, d))


def _ffn_up_kernel(h_ref, wg_ref, wu_ref, cwg_ref, cwu_ref, cbg_ref, cbu_ref, o_ref, wgb_ref, wub_ref,
                   gbuf_ref, ubuf_ref, *, tm):
    seq = h_ref.shape[0]
    H = CONV_HALO

    @pl.when(pl.program_id(1) == 0)
    def _():
        wgb_ref[...] = wg_ref[...].astype(BF16)
        wub_ref[...] = wu_ref[...].astype(BF16)
        gbuf_ref[0:H, :] = jnp.zeros((H, gbuf_ref.shape[1]), F32)
        ubuf_ref[0:H, :] = jnp.zeros((H, ubuf_ref.shape[1]), F32)

    def conv(buf_ref, cw_ref, cb_ref, r0):
        acc = cb_ref[...]
        for k in range(FFN_CONV):
            start = H + r0 - (FFN_CONV - 1) + k
            acc = acc + cw_ref[k:k + 1, :] * buf_ref[start:start + tm, :]
        return acc

    for r0 in range(0, seq, tm):
        hmat = h_ref[r0:r0 + tm, :]
        gbuf_ref[H + r0:H + r0 + tm, :] = _dot(hmat, wgb_ref[...])
        ubuf_ref[H + r0:H + r0 + tm, :] = _dot(hmat, wub_ref[...])
        gate = conv(gbuf_ref, cwg_ref, cbg_ref, r0)
        up = conv(ubuf_ref, cwu_ref, cbu_ref, r0)
        o_ref[r0:r0 + tm, :] = (jax.nn.gelu(gate, approximate=True) * up).astype(o_ref.dtype)


def _ffn_up(h, w_up, conv_w, conv_b, layer, seq, tm=512, tn=512):
    t, d = h.shape
    nt = FFN_HIDDEN // tn
    col = lambda rows, half: pl.BlockSpec((None, rows, tn), lambda j, b: (layer, 0, half * nt + j))
    return pl.pallas_call(
        functools.partial(_ffn_up_kernel, tm=tm),
        grid=(nt, t // seq),
        in_specs=[pl.BlockSpec((seq, d), lambda j, b: (b, 0)),
                  col(d, 0), col(d, 1), col(FFN_CONV, 0), col(FFN_CONV, 1), col(1, 0), col(1, 1)],
        out_specs=pl.BlockSpec((seq, tn), lambda j, b: (b, j)),
        out_shape=jax.ShapeDtypeStruct((t, FFN_HIDDEN), BF16),
        scratch_shapes=[pltpu.VMEM((d, tn), BF16), pltpu.VMEM((d, tn), BF16),
                        pltpu.VMEM((seq + CONV_HALO, tn), F32), pltpu.VMEM((seq + CONV_HALO, tn), F32)],
        compiler_params=_params("parallel", "arbitrary"),
        name="ffn_up_conv_geglu",
    )(h, w_up, w_up, conv_w, conv_w, conv_b, conv_b)


def _ffn_down_kernel(a_ref, w_ref, x_ref, gpost_ref, gnext_ref, xo_ref, ho_ref, acc_ref):
    k = pl.program_id(1)

    @pl.when(k == 0)
    def _():
        acc_ref[...] = jnp.zeros(acc_ref.shape, F32)

    acc_ref[...] += _dot(a_ref[...], w_ref[...])

    @pl.when(k == pl.num_programs(1) - 1)
    def _():
        x_new = x_ref[...] + _rms(acc_ref[...], gpost_ref[...])
        xo_ref[...] = x_new
        ho_ref[...] = _rms(x_new, gnext_ref[...]).astype(ho_ref.dtype)


def _ffn_down(act, w, x, g_post, g_next, tm=512, tk=1408):
    t, d = x.shape
    kdim = act.shape[1]
    row = pl.BlockSpec((1, d), lambda m, k: (0, 0))
    return pl.pallas_call(
        _ffn_down_kernel,
        grid=(t // tm, kdim // tk),
        in_specs=[pl.BlockSpec((tm, tk), lambda m, k: (m, k)),
                  pl.BlockSpec((tk, d), lambda m, k: (k, 0)),
                  pl.BlockSpec((tm, d), lambda m, k: (m, 0)),
                  row, row],
        out_specs=[pl.BlockSpec((tm, d), lambda m, k: (m, 0)), pl.BlockSpec((tm, d), lambda m, k: (m, 0))],
        out_shape=[jax.ShapeDtypeStruct((t, d), F32), jax.ShapeDtypeStruct((t, d), BF16)],
        scratch_shapes=[pltpu.VMEM((tm, d), F32)],
        compiler_params=_params("parallel", "arbitrary"),
        name="ffn_down",
    )(act, w, x, g_post.reshape(1, d), g_next.reshape(1, d))


def kernel(x, rel_bias, ln_mix_pre, w_in, ssm_conv_w, ssm_conv_b, dt_bias, a_log, d_skip, ssm_norm_w, w_out,
           ln_mix_post, ln_ffn_pre, w_ffn_up, ffn_conv_w, ffn_conv_b, w_ffn_down, ln_ffn_post):
    bsz, seq, d = x.shape
    depth = w_in.shape[0]
    t = bsz * seq
    assert w_in.shape[2] == QKV_COLS + REST_COLS + SSM_HEADS

    rel_bias_t = rel_bias.T.astype(F32)
    town, tprev = _bias_tiles(rel_bias_t)
    w_dt = jnp.pad(w_in[:, :, QKV_COLS + REST_COLS:], ((0, 0), (0, 0), (0, LANES - SSM_HEADS)))
    ffn_conv_b3 = ffn_conv_b.reshape(depth, 1, -1)
    qkv_tn, rest_tn = 1024, 512

    xf = x.reshape(t, d)
    h = _rmsnorm(xf, ln_mix_pre[0])
    for l in range(depth):
        qkv = _matmul(h, w_in, l, 0, QKV_COLS // qkv_tn, BF16, tm=1024, tn=qkv_tn, name="in_proj_qkv")
        rest = _matmul(h, w_in, l, QKV_COLS // rest_tn, REST_COLS // rest_tn, F32, tm=1024, tn=rest_tn,
                       name="in_proj_ssm")
        dt_raw = _matmul(h, w_dt, l, 0, 1, F32, tm=1024, tn=LANES, name="in_proj_dt")
        attn = _moba(qkv.reshape(bsz, seq, QKV_COLS), rel_bias_t, town, tprev, bsz, seq)
        ssm = _ssd(rest.reshape(bsz, seq, REST_COLS), dt_raw.reshape(bsz, seq, LANES), ssm_conv_w[l],
                   ssm_conv_b[l], dt_bias[l], a_log[l], d_skip[l], ssm_norm_w[l], bsz, seq)
        xf, h = _outproj(attn.reshape(t, ATTN_WIDTH), ssm.reshape(t, SSM_WIDTH), w_out[l].astype(BF16), xf,
                         ln_mix_post[l], ln_ffn_pre[l])
        act = _ffn_up(h, w_ffn_up, ffn_conv_w, ffn_conv_b3, l, seq)
        g_next = ln_mix_pre[(l + 1) % depth]
        xf, h = _ffn_down(act, w_ffn_down[l].astype(BF16), xf, ln_ffn_post[l], g_next)
    return xf.reshape(bsz, seq, d)
```

```python
import functools
import math

import jax
import jax.numpy as jnp
import numpy as np
from jax import lax
from jax.experimental import pallas as pl
from jax.experimental.pallas import tpu as pltpu

F32 = jnp.float32
BF16 = jnp.bfloat16

LANES = 128
SUBLANES = 8
VMEM_LIMIT_BYTES = 56 * 1024 * 1024

D_MODEL = 2048
ATTN_HEAD_DIM = 128
ATTN_WIDTH = 1024
ATTN_HEADS = ATTN_WIDTH // ATTN_HEAD_DIM
MOBA_BLOCK = 256
MOBA_TOPK = 3
REL_BUCKETS = 32
REL_MAX_DIST = 128
SSM_WIDTH = 1024
SSM_HEAD_DIM = 64
SSM_HEADS = SSM_WIDTH // SSM_HEAD_DIM
SSM_GROUPS = 2
SSM_GROUP_WIDTH = SSM_WIDTH // SSM_GROUPS
SSM_STATE = 128
SSM_CONV = 4
SSM_CHUNK = 128
SSM_BC_WIDTH = 2 * SSM_GROUPS * SSM_STATE
FFN_HIDDEN = 5632
FFN_CONV = 3
NORM_EPS = 1e-6
NEG = -1e30

QKV_COLS = 3 * ATTN_WIDTH
REST_COLS = SSM_WIDTH + SSM_WIDTH + SSM_BC_WIDTH
CONV_HALO = SUBLANES


def _params(*semantics):
    return pltpu.CompilerParams(dimension_semantics=semantics, vmem_limit_bytes=VMEM_LIMIT_BYTES)


def _rms(x, w):
    return x * lax.rsqrt(jnp.mean(x * x, axis=-1, keepdims=True) + NORM_EPS) * w


def _split3(x):
    hi = x.astype(BF16)
    r1 = x - hi.astype(F32)
    mid = r1.astype(BF16)
    lo = (r1 - mid.astype(F32)).astype(BF16)
    return hi, mid, lo


def _dot(a, b):
    return jnp.dot(a, b, preferred_element_type=F32)


def _dot_nt(a, b):
    return lax.dot_general(a, b, (((1,), (1,)), ((), ())), preferred_element_type=F32)


def _rmsnorm_kernel(x_ref, w_ref, o_ref):
    o_ref[...] = _rms(x_ref[...], w_ref[...]).astype(o_ref.dtype)


def _rmsnorm(x, w, tm=512):
    t, d = x.shape
    return pl.pallas_call(
        _rmsnorm_kernel,
        grid=(t // tm,),
        in_specs=[pl.BlockSpec((tm, d), lambda m: (m, 0)),
                  pl.BlockSpec((1, d), lambda m: (0, 0))],
        out_specs=pl.BlockSpec((tm, d), lambda m: (m, 0)),
        out_shape=jax.ShapeDtypeStruct((t, d), BF16),
        compiler_params=_params("parallel"),
        name="rmsnorm",
    )(x, w.reshape(1, d))


def _matmul_kernel(a_ref, wt_ref, o_ref, wb_ref):
    @pl.when(pl.program_id(1) == 0)
    def _():
        wb_ref[...] = wt_ref[...].T.astype(BF16)

    o_ref[...] = _dot(a_ref[...], wb_ref[...]).astype(o_ref.dtype)


def _matmul(a, wt, layer, first_tile, n_tiles, out_dtype, tm, tn, name):
    t, k = a.shape
    return pl.pallas_call(
        _matmul_kernel,
        grid=(n_tiles, t // tm),
        in_specs=[pl.BlockSpec((tm, k), lambda j, m: (m, 0)),
                  pl.BlockSpec((None, tn, k), lambda j, m: (layer, first_tile + j, 0))],
        out_specs=pl.BlockSpec((tm, tn), lambda j, m: (m, j)),
        out_shape=jax.ShapeDtypeStruct((t, n_tiles * tn), out_dtype),
        scratch_shapes=[pltpu.VMEM((k, tn), BF16)],
        compiler_params=_params("parallel", "arbitrary"),
        name=name,
    )(a, wt)


def _t5_bucket_np(rel):
    n = np.maximum(rel, 0)
    max_exact = REL_BUCKETS // 2
    nf = np.maximum(n, 1).astype(np.float32)
    large = max_exact + (np.log(nf / np.float32(max_exact)) / np.float32(math.log(REL_MAX_DIST / max_exact))
                         * np.float32(REL_BUCKETS - max_exact)).astype(np.int32)
    large = np.minimum(large, REL_BUCKETS - 1)
    return np.where(n < max_exact, n, large).astype(np.int32)


def _bias_tiles_kernel(rb_ref, bown_ref, bprev_ref, town_ref, tprev_ref):
    h = pl.program_id(0)
    bown = bown_ref[...]
    bprev = bprev_ref[...]
    town = jnp.full(bown.shape, NEG, F32)
    tprev = jnp.zeros(bprev.shape, F32)
    for b in range(REL_BUCKETS):
        val = rb_ref[h, b]
        town = jnp.where(bown == b, val, town)
        tprev = jnp.where(bprev == b, val, tprev)
    town_ref[...] = town
    tprev_ref[...] = tprev


def _bias_tiles(rel_bias_t):
    r = np.arange(MOBA_BLOCK)
    rel_own = r[None, :] - r[:, None]
    bown = np.where(rel_own >= 0, _t5_bucket_np(rel_own), -1).astype(np.int32)
    bprev = _t5_bucket_np(MOBA_BLOCK + rel_own)
    tile = pl.BlockSpec((MOBA_BLOCK, MOBA_BLOCK), lambda h: (0, 0))
    out_tile = pl.BlockSpec((None, MOBA_BLOCK, MOBA_BLOCK), lambda h: (h, 0, 0))
    shape = jax.ShapeDtypeStruct((ATTN_HEADS, MOBA_BLOCK, MOBA_BLOCK), F32)
    return pl.pallas_call(
        _bias_tiles_kernel,
        grid=(ATTN_HEADS,),
        in_specs=[pl.BlockSpec(memory_space=pltpu.SMEM), tile, tile],
        out_specs=[out_tile, out_tile],
        out_shape=[shape, shape],
        compiler_params=_params("parallel"),
        name="t5_bias_tiles",
    )(rel_bias_t, jnp.asarray(bown), jnp.asarray(bprev))


KM_ROWS = 2 * SUBLANES


def _moba_kernel(rb_ref, q_ref, k_ref, v_ref, town_ref, tprev_ref, o_ref, vt_ref, km_ref, s_ref):
    h = pl.program_id(1)
    nb = k_ref.shape[0] // MOBA_BLOCK
    scale = ATTN_HEAD_DIM ** -0.5
    far_bias = rb_ref[h, REL_BUCKETS - 1]
    blocks = [slice(jb * MOBA_BLOCK, (jb + 1) * MOBA_BLOCK) for jb in range(nb)]

    km_ref[...] = jnp.zeros(km_ref.shape, F32)
    for jb in range(nb):
        vt_ref[:, blocks[jb]] = v_ref[blocks[jb], :].astype(F32).T.astype(BF16)
        km_ref[jb:jb + 1, :] = jnp.mean(k_ref[blocks[jb], :].astype(F32), axis=0, keepdims=True)
    km = km_ref[...]
    km_hi = km.astype(BF16)
    km_lo = (km - km_hi.astype(F32)).astype(BF16)

    for i in range(nb):
        q = q_ref[blocks[i], :]
        gated = i > MOBA_TOPK
        if gated:
            gate_t = (_dot_nt(km_hi, q) + _dot_nt(km_lo, q))[0:SUBLANES, :]
            blk = lax.broadcasted_iota(jnp.int32, gate_t.shape, 0)
            rank = jnp.zeros(gate_t.shape, jnp.int32)
            for jp in range(i):
                row = gate_t[jp:jp + 1, :]
                beats = (row > gate_t) | ((row == gate_t) & (jp < blk))
                rank = rank + jnp.where(beats, 1, 0)
            mneg_t = jnp.where((rank < MOBA_TOPK) & (blk < i), 0.0, NEG)

        buf = i % 2
        m = None
        for j in range(i + 1):
            s = _dot_nt(k_ref[blocks[j], :], q) * scale
            if j == i:
                s = s + town_ref[...]
            else:
                mrow = mneg_t[j:j + 1, :] if gated else None
                if j == i - 1:
                    s = s + tprev_ref[...]
                    if gated:
                        s = s + mrow
                else:
                    s = s + (far_bias + mrow if gated else far_bias)
            s_ref[buf, j] = s
            cmax = jnp.max(s, axis=0, keepdims=True)
            m = cmax if m is None else jnp.maximum(m, cmax)

        l = jnp.zeros((1, MOBA_BLOCK), F32)
        acc = jnp.zeros((ATTN_HEAD_DIM, MOBA_BLOCK), F32)
        for j in range(i + 1):
            p = jnp.exp(s_ref[buf, j] - m)
            l = l + jnp.sum(p, axis=0, keepdims=True)
            acc = acc + _dot(vt_ref[:, blocks[j]], p.astype(BF16))
        o_ref[blocks[i], :] = (acc / l).T.astype(o_ref.dtype)


def _moba(qkv, rel_bias_t, town, tprev, bsz, seq):
    nb = seq // MOBA_BLOCK
    assert (MOBA_BLOCK + MOBA_BLOCK) > REL_MAX_DIST and nb <= SUBLANES
    tile = pl.BlockSpec((None, MOBA_BLOCK, MOBA_BLOCK), lambda b, h: (h, 0, 0))
    head_cols = lambda part: pl.BlockSpec((None, seq, ATTN_HEAD_DIM), lambda b, h: (b, 0, part * ATTN_HEADS + h))
    return pl.pallas_call(
        _moba_kernel,
        grid=(bsz, ATTN_HEADS),
        in_specs=[pl.BlockSpec(memory_space=pltpu.SMEM), head_cols(0), head_cols(1), head_cols(2), tile, tile],
        out_specs=pl.BlockSpec((None, seq, ATTN_HEAD_DIM), lambda b, h: (b, 0, h)),
        out_shape=jax.ShapeDtypeStruct((bsz, seq, ATTN_WIDTH), BF16),
        scratch_shapes=[pltpu.VMEM((ATTN_HEAD_DIM, seq), BF16),
                        pltpu.VMEM((KM_ROWS, ATTN_HEAD_DIM), F32),
                        pltpu.VMEM((2, nb, MOBA_BLOCK, MOBA_BLOCK), F32)],
        compiler_params=_params("parallel", "parallel"),
        name="moba_attention",
    )(rel_bias_t, qkv, qkv, qkv, town, tprev)


def _silu(x):
    return x * jax.nn.sigmoid(x)


def _ssd_kernel(z_ref, xs_ref, bc_ref, dt_ref, cwx_ref, cwbc_ref, cbx_ref, cbbc_ref, dtb_ref, alog_ref,
                dskip_ref, normw_ref, expand_ref, ltri_ref, o_ref, xbuf_ref, bcbuf_ref, st_ref, y_ref):
    c = pl.program_id(1)
    L = SSM_CHUNK
    H = CONV_HALO

    @pl.when(c == 0)
    def _():
        xbuf_ref[0:H, :] = jnp.zeros((H, xbuf_ref.shape[1]), F32)
        bcbuf_ref[0:H, :] = jnp.zeros((H, bcbuf_ref.shape[1]), F32)
        st_ref[...] = jnp.zeros(st_ref.shape, F32)

    xbuf_ref[H:H + L, :] = xs_ref[...]
    bcbuf_ref[H:H + L, :] = bc_ref[...]

    def conv_silu(buf_ref, w_ref, b_ref):
        acc = b_ref[...]
        for k in range(SSM_CONV):
            start = H - (SSM_CONV - 1) + k
            acc = acc + w_ref[k:k + 1, :] * buf_ref[start:start + L, :]
        return _silu(acc)

    xs = conv_silu(xbuf_ref, cwx_ref, cbx_ref)
    bc = conv_silu(bcbuf_ref, cwbc_ref, cbbc_ref)
    xbuf_ref[0:H, :] = xbuf_ref[L:L + H, :]
    bcbuf_ref[0:H, :] = bcbuf_ref[L:L + H, :]

    x_dt = dt_ref[...] + dtb_ref[...]
    dt = jnp.maximum(x_dt, 0.0) + jnp.log1p(jnp.exp(-jnp.abs(x_dt)))
    da = dt * (-jnp.exp(alog_ref[...]))
    ltri = ltri_ref[...]
    a_cs = sum(_dot(ltri, part) for part in _split3(da))
    a_last = a_cs[L - 1:L, :]
    decay_st = jnp.exp(a_last - a_cs)
    exp_acs = jnp.exp(a_cs)
    chunk_decay = jnp.broadcast_to(jnp.exp(a_last), (2 * SUBLANES, LANES))

    stacked = jnp.concatenate([dt, decay_st, exp_acs, chunk_decay], axis=0)
    expand = expand_ref[...]
    wide = sum(_dot(part, expand) for part in _split3(stacked))
    dt_w = wide[0:L]
    decay_st_w = wide[L:2 * L]
    exp_acs_w = wide[2 * L:3 * L]
    chunk_decay_w = wide[3 * L:3 * L + 1]

    xc = xs * dt_w
    a_cs_t = a_cs.T
    row_i = lax.broadcasted_iota(jnp.int32, (L, L), 0)
    col_i = lax.broadcasted_iota(jnp.int32, (L, L), 1)
    causal = row_i >= col_i
    first_head = col_i < SSM_HEAD_DIM
    heads_per_group = SSM_HEADS // SSM_GROUPS

    for g in range(SSM_GROUPS):
        gsl = slice(g * SSM_GROUP_WIDTH, (g + 1) * SSM_GROUP_WIDTH)
        bg = bc[:, g * SSM_STATE:(g + 1) * SSM_STATE]
        cg = bc[:, (SSM_GROUPS + g) * SSM_STATE:(SSM_GROUPS + g + 1) * SSM_STATE].astype(BF16)
        cb = _dot_nt(cg, bg.astype(BF16))
        prev = st_ref[:, gsl]
        y_off = _dot(cg, prev.astype(BF16)) * exp_acs_w[:, gsl]
        st_ref[:, gsl] = prev * chunk_decay_w[:, gsl] + _dot(bg.T.astype(BF16),
                                                             (xc[:, gsl] * decay_st_w[:, gsl]).astype(BF16))
        for pr in range(heads_per_group // 2):
            h0 = g * heads_per_group + 2 * pr
            psl = slice(h0 * SSM_HEAD_DIM, (h0 + 2) * SSM_HEAD_DIM)
            xcp = xc[:, psl].astype(BF16)
            ys = []
            for hh in (h0, h0 + 1):
                seg = a_cs[:, hh:hh + 1] - a_cs_t[hh:hh + 1, :]
                decay = jnp.exp(jnp.where(causal, seg, NEG))
                ys.append(_dot((cb * decay).astype(BF16), xcp))
            y_ref[:, psl] = jnp.where(first_head, ys[0], ys[1]) + y_off[:, 2 * pr * SSM_HEAD_DIM:
                                                                         (2 * pr + 2) * SSM_HEAD_DIM]

    y = (y_ref[...] + dskip_ref[...] * xs) * _silu(z_ref[...])
    normw = normw_ref[...]
    for g in range(SSM_GROUPS):
        gsl = slice(g * SSM_GROUP_WIDTH, (g + 1) * SSM_GROUP_WIDTH)
        yg = y[:, gsl]
        yg = yg * lax.rsqrt(jnp.mean(yg * yg, axis=-1, keepdims=True) + NORM_EPS)
        o_ref[:, gsl] = (yg * normw[:, gsl]).astype(o_ref.dtype)


def _ssd(rest, dt_raw, conv_w, conv_b, dt_bias, a_log, d_skip, norm_w, bsz, seq):
    L = SSM_CHUNK
    nc = seq // L
    pad_heads = LANES - SSM_HEADS
    expand = np.zeros((LANES, SSM_WIDTH), np.float32)
    for hh in range(SSM_HEADS):
        expand[hh, hh * SSM_HEAD_DIM:(hh + 1) * SSM_HEAD_DIM] = 1.0
    ltri = np.tril(np.ones((L, L), np.float32))

    row = lambda v: v.reshape(1, -1).astype(F32)
    const = lambda shape: pl.BlockSpec(shape, lambda b, c: (0, 0))
    args = (rest, rest, rest, dt_raw,
            conv_w[:, :SSM_WIDTH], conv_w[:, SSM_WIDTH:], row(conv_b[:SSM_WIDTH]), row(conv_b[SSM_WIDTH:]),
            row(jnp.pad(dt_bias, (0, pad_heads))), row(jnp.pad(a_log, (0, pad_heads))),
            row(jnp.repeat(d_skip, SSM_HEAD_DIM)), row(norm_w),
            jnp.asarray(expand, BF16), jnp.asarray(ltri, BF16))
    in_specs = [
        pl.BlockSpec((None, L, SSM_WIDTH), lambda b, c: (b, c, 0)),
        pl.BlockSpec((None, L, SSM_WIDTH), lambda b, c: (b, c, 1)),
        pl.BlockSpec((None, L, SSM_BC_WIDTH), lambda b, c: (b, c, 2 * SSM_WIDTH // SSM_BC_WIDTH)),
        pl.BlockSpec((None, L, LANES), lambda b, c: (b, c, 0)),
        const((SSM_CONV, SSM_WIDTH)), const((SSM_CONV, SSM_BC_WIDTH)),
        const((1, SSM_WIDTH)), const((1, SSM_BC_WIDTH)),
        const((1, LANES)), const((1, LANES)), const((1, SSM_WIDTH)), const((1, SSM_WIDTH)),
        const((LANES, SSM_WIDTH)), const((L, L)),
    ]
    return pl.pallas_call(
        _ssd_kernel,
        grid=(bsz, nc),
        in_specs=in_specs,
        out_specs=pl.BlockSpec((None, L, SSM_WIDTH), lambda b, c: (b, c, 0)),
        out_shape=jax.ShapeDtypeStruct((bsz, seq, SSM_WIDTH), BF16),
        scratch_shapes=[pltpu.VMEM((L + CONV_HALO, SSM_WIDTH), F32),
                        pltpu.VMEM((L + CONV_HALO, SSM_BC_WIDTH), F32),
                        pltpu.VMEM((SSM_STATE, SSM_WIDTH), F32),
                        pltpu.VMEM((L, SSM_WIDTH), F32)],
        compiler_params=_params("parallel", "arbitrary"),
        name="ssd_mixer",
    )(*args)


def _outproj_kernel(attn_ref, ssm_ref, w_ref, x_ref, gpost_ref, gnext_ref, xo_ref, ho_ref, wb_ref, *, sub):
    @pl.when(pl.program_id(0) == 0)
    def _():
        wb_ref[...] = w_ref[...].astype(BF16)

    for r0 in range(0, x_ref.shape[0], sub):
        rows = slice(r0, r0 + sub)
        mixed = _dot(attn_ref[rows, :], wb_ref[0:ATTN_WIDTH, :]) + _dot(ssm_ref[rows, :], wb_ref[ATTN_WIDTH:, :])
        x_new = x_ref[rows, :] + _rms(mixed, gpost_ref[...])
        xo_ref[rows, :] = x_new
        ho_ref[rows, :] = _rms(x_new, gnext_ref[...]).astype(ho_ref.dtype)


def _outproj(attn, ssm, w, layer, x, g_post, g_next, tm=512, sub=256):
    t, d = x.shape
    row = pl.BlockSpec((1, d), lambda m: (0, 0))
    return pl.pallas_call(
        functools.partial(_outproj_kernel, sub=sub),
        grid=(t // tm,),
        in_specs=[pl.BlockSpec((tm, ATTN_WIDTH), lambda m: (m, 0)),
                  pl.BlockSpec((tm, SSM_WIDTH), lambda m: (m, 0)),
                  pl.BlockSpec((None,) + w.shape[1:], lambda m: (layer, 0, 0), pipeline_mode=pl.Buffered(1)),
                  pl.BlockSpec((tm, d), lambda m: (m, 0)),
                  row, row],
        out_specs=[pl.BlockSpec((tm, d), lambda m: (m, 0)), pl.BlockSpec((tm, d), lambda m: (m, 0))],
        out_shape=[jax.ShapeDtypeStruct((t, d), F32), jax.ShapeDtypeStruct((t, d), BF16)],
        scratch_shapes=[pltpu.VMEM(w.shape[1:], BF16)],
        compiler_params=_params("arbitrary"),
        name="out_proj",
    )(attn, ssm, w, x, g_post.reshape(1, d), g_next.reshape(1, d))


def _ffn_up_kernel(h_ref, wg_ref, wu_ref, cwg_ref, cwu_ref, cbg_ref, cbu_ref, o_ref, wgb_ref, wub_ref,
                   gbuf_ref, ubuf_ref, *, tm):
    seq = h_ref.shape[0]
    H = CONV_HALO

    @pl.when(pl.program_id(1) == 0)
    def _():
        wgb_ref[...] = wg_ref[...].astype(BF16)
        wub_ref[...] = wu_ref[...].astype(BF16)
        gbuf_ref[0:H, :] = jnp.zeros((H, gbuf_ref.shape[1]), F32)
        ubuf_ref[0:H, :] = jnp.zeros((H, ubuf_ref.shape[1]), F32)

    def conv(buf_ref, cw_ref, cb_ref, r0):
        acc = cb_ref[...]
        for k in range(FFN_CONV):
            start = H + r0 - (FFN_CONV - 1) + k
            acc = acc + cw_ref[k:k + 1, :] * buf_ref[start:start + tm, :]
        return acc

    for r0 in range(0, seq, tm):
        hmat = h_ref[r0:r0 + tm, :]
        gbuf_ref[H + r0:H + r0 + tm, :] = _dot(hmat, wgb_ref[...])
        ubuf_ref[H + r0:H + r0 + tm, :] = _dot(hmat, wub_ref[...])
        gate = conv(gbuf_ref, cwg_ref, cbg_ref, r0)
        up = conv(ubuf_ref, cwu_ref, cbu_ref, r0)
        o_ref[r0:r0 + tm, :] = (jax.nn.gelu(gate, approximate=True) * up).astype(o_ref.dtype)


def _ffn_up(h, w_up, conv_w, conv_b, layer, seq, tm=512, tn=512):
    t, d = h.shape
    nt = FFN_HIDDEN // tn
    col = lambda rows, half: pl.BlockSpec((None, rows, tn), lambda j, b: (layer, 0, half * nt + j))
    return pl.pallas_call(
        functools.partial(_ffn_up_kernel, tm=tm),
        grid=(nt, t // seq),
        in_specs=[pl.BlockSpec((seq, d), lambda j, b: (b, 0)),
                  col(d, 0), col(d, 1), col(FFN_CONV, 0), col(FFN_CONV, 1), col(1, 0), col(1, 1)],
        out_specs=pl.BlockSpec((seq, tn), lambda j, b: (b, j)),
        out_shape=jax.ShapeDtypeStruct((t, FFN_HIDDEN), BF16),
        scratch_shapes=[pltpu.VMEM((d, tn), BF16), pltpu.VMEM((d, tn), BF16),
                        pltpu.VMEM((seq + CONV_HALO, tn), F32), pltpu.VMEM((seq + CONV_HALO, tn), F32)],
        compiler_params=_params("parallel", "arbitrary"),
        name="ffn_up_conv_geglu",
    )(h, w_up, w_up, conv_w, conv_w, conv_b, conv_b)


def _ffn_down_kernel(a_ref, w_ref, x_ref, gpost_ref, gnext_ref, xo_ref, ho_ref, acc_ref):
    k = pl.program_id(1)

    @pl.when(k == 0)
    def _():
        acc_ref[...] = jnp.zeros(acc_ref.shape, F32)

    acc_ref[...] += _dot(a_ref[...], w_ref[...])

    @pl.when(k == pl.num_programs(1) - 1)
    def _():
        x_new = x_ref[...] + _rms(acc_ref[...], gpost_ref[...])
        xo_ref[...] = x_new
        ho_ref[...] = _rms(x_new, gnext_ref[...]).astype(ho_ref.dtype)


def _ffn_down(act, w, x, g_post, g_next, tm=512, tk=2816):
    t, d = x.shape
    kdim = act.shape[1]
    row = pl.BlockSpec((1, d), lambda m, k: (0, 0))
    return pl.pallas_call(
        _ffn_down_kernel,
        grid=(t // tm, kdim // tk),
        in_specs=[pl.BlockSpec((tm, tk), lambda m, k: (m, k)),
                  pl.BlockSpec((tk, d), lambda m, k: (k, 0)),
                  pl.BlockSpec((tm, d), lambda m, k: (m, 0)),
                  row, row],
        out_specs=[pl.BlockSpec((tm, d), lambda m, k: (m, 0)), pl.BlockSpec((tm, d), lambda m, k: (m, 0))],
        out_shape=[jax.ShapeDtypeStruct((t, d), F32), jax.ShapeDtypeStruct((t, d), BF16)],
        scratch_shapes=[pltpu.VMEM((tm, d), F32)],
        compiler_params=_params("parallel", "arbitrary"),
        name="ffn_down",
    )(act, w, x, g_post.reshape(1, d), g_next.reshape(1, d))


def kernel(x, rel_bias, ln_mix_pre, w_in, ssm_conv_w, ssm_conv_b, dt_bias, a_log, d_skip, ssm_norm_w, w_out,
           ln_mix_post, ln_ffn_pre, w_ffn_up, ffn_conv_w, ffn_conv_b, w_ffn_down, ln_ffn_post):
    bsz, seq, d = x.shape
    depth = w_in.shape[0]
    t = bsz * seq
    assert w_in.shape[2] == QKV_COLS + REST_COLS + SSM_HEADS

    rel_bias_t = rel_bias.T.astype(F32)
    town, tprev = _bias_tiles(rel_bias_t)
    w_in_t = jnp.swapaxes(w_in, 1, 2)
    w_dt_t = jnp.pad(w_in_t[:, QKV_COLS + REST_COLS:, :], ((0, 0), (0, LANES - SSM_HEADS), (0, 0)))
    ffn_conv_b3 = ffn_conv_b.reshape(depth, 1, -1)
    qkv_tn, rest_tn = 1024, 512

    xf = x.reshape(t, d)
    h = _rmsnorm(xf, ln_mix_pre[0])
    for l in range(depth):
        qkv = _matmul(h, w_in_t, l, 0, QKV_COLS // qkv_tn, BF16, tm=1024, tn=qkv_tn, name="in_proj_qkv")
        rest = _matmul(h, w_in_t, l, QKV_COLS // rest_tn, REST_COLS // rest_tn, F32, tm=1024, tn=rest_tn,
                       name="in_proj_ssm")
        dt_raw = _matmul(h, w_dt_t, l, 0, 1, F32, tm=1024, tn=LANES, name="in_proj_dt")
        attn = _moba(qkv.reshape(bsz, seq, QKV_COLS), rel_bias_t, town, tprev, bsz, seq)
        ssm = _ssd(rest.reshape(bsz, seq, REST_COLS), dt_raw.reshape(bsz, seq, LANES), ssm_conv_w[l],
                   ssm_conv_b[l], dt_bias[l], a_log[l], d_skip[l], ssm_norm_w[l], bsz, seq)
        xf, h = _outproj(attn.reshape(t, ATTN_WIDTH), ssm.reshape(t, SSM_WIDTH), w_out, l, xf,
                         ln_mix_post[l], ln_ffn_pre[l])
        act = _ffn_up(h, w_ffn_up, ffn_conv_w, ffn_conv_b3, l, seq)
        g_next = ln_mix_pre[(l + 1) % depth]
        xf, h = _ffn_down(act, w_ffn_down[l].astype(BF16), xf, ln_ffn_post[l], g_next)
    return xf.reshape(bsz, seq, d)
```

```python
import functools
import math

import jax
import jax.numpy as jnp
import numpy as np
from jax import lax
from jax.experimental import pallas as pl
from jax.experimental.pallas import tpu as pltpu

F32 = jnp.float32
BF16 = jnp.bfloat16

LANES = 128
SUBLANES = 8
VMEM_LIMIT_BYTES = 56 * 1024 * 1024

D_MODEL = 2048
ATTN_HEAD_DIM = 128
ATTN_WIDTH = 1024
ATTN_HEADS = ATTN_WIDTH // ATTN_HEAD_DIM
MOBA_BLOCK = 256
MOBA_TOPK = 3
REL_BUCKETS = 32
REL_MAX_DIST = 128
SSM_WIDTH = 1024
SSM_HEAD_DIM = 64
SSM_HEADS = SSM_WIDTH // SSM_HEAD_DIM
SSM_GROUPS = 2
SSM_GROUP_WIDTH = SSM_WIDTH // SSM_GROUPS
SSM_STATE = 128
SSM_CONV = 4
SSM_CHUNK = 128
SSM_BC_WIDTH = 2 * SSM_GROUPS * SSM_STATE
FFN_HIDDEN = 5632
FFN_CONV = 3
NORM_EPS = 1e-6
NEG = -1e30
LOG2E = math.log2(math.e)

QKV_COLS = 3 * ATTN_WIDTH
REST_COLS = SSM_WIDTH + SSM_WIDTH + SSM_BC_WIDTH
CONV_HALO = SUBLANES


def _params(*semantics):
    return pltpu.CompilerParams(dimension_semantics=semantics, vmem_limit_bytes=VMEM_LIMIT_BYTES)


def _rms(x, w):
    return x * lax.rsqrt(jnp.mean(x * x, axis=-1, keepdims=True) + NORM_EPS) * w


def _split3(x):
    hi = x.astype(BF16)
    r1 = x - hi.astype(F32)
    mid = r1.astype(BF16)
    lo = (r1 - mid.astype(F32)).astype(BF16)
    return hi, mid, lo


def _dot(a, b):
    return jnp.dot(a, b, preferred_element_type=F32)


def _dot_nt(a, b):
    return lax.dot_general(a, b, (((1,), (1,)), ((), ())), preferred_element_type=F32)


def _rmsnorm_kernel(x_ref, w_ref, o_ref):
    o_ref[...] = _rms(x_ref[...], w_ref[...]).astype(o_ref.dtype)


def _rmsnorm(x, w, tm=512):
    t, d = x.shape
    return pl.pallas_call(
        _rmsnorm_kernel,
        grid=(t // tm,),
        in_specs=[pl.BlockSpec((tm, d), lambda m: (m, 0)),
                  pl.BlockSpec((1, d), lambda m: (0, 0))],
        out_specs=pl.BlockSpec((tm, d), lambda m: (m, 0)),
        out_shape=jax.ShapeDtypeStruct((t, d), BF16),
        compiler_params=_params("parallel"),
        name="rmsnorm",
    )(x, w.reshape(1, d))


def _matmul_kernel(a_ref, wt_ref, o_ref, wb_ref):
    @pl.when(pl.program_id(1) == 0)
    def _():
        wb_ref[...] = wt_ref[...].T.astype(BF16)

    o_ref[...] = _dot(a_ref[...], wb_ref[...]).astype(o_ref.dtype)


def _matmul(a, wt, layer, first_tile, n_tiles, out_dtype, tm, tn, name):
    t, k = a.shape
    return pl.pallas_call(
        _matmul_kernel,
        grid=(n_tiles, t // tm),
        in_specs=[pl.BlockSpec((tm, k), lambda j, m: (m, 0)),
                  pl.BlockSpec((None, tn, k), lambda j, m: (layer, first_tile + j, 0))],
        out_specs=pl.BlockSpec((tm, tn), lambda j, m: (m, j)),
        out_shape=jax.ShapeDtypeStruct((t, n_tiles * tn), out_dtype),
        scratch_shapes=[pltpu.VMEM((k, tn), BF16)],
        compiler_params=_params("parallel", "arbitrary"),
        name=name,
    )(a, wt)


def _t5_bucket_np(rel):
    n = np.maximum(rel, 0)
    max_exact = REL_BUCKETS // 2
    nf = np.maximum(n, 1).astype(np.float32)
    large = max_exact + (np.log(nf / np.float32(max_exact)) / np.float32(math.log(REL_MAX_DIST / max_exact))
                         * np.float32(REL_BUCKETS - max_exact)).astype(np.int32)
    large = np.minimum(large, REL_BUCKETS - 1)
    return np.where(n < max_exact, n, large).astype(np.int32)


def _bias_tiles_kernel(rb_ref, bown_ref, bprev_ref, town_ref, tprev_ref):
    h = pl.program_id(0)
    bown = bown_ref[...]
    bprev = bprev_ref[...]
    town = jnp.full(bown.shape, NEG, F32)
    tprev = jnp.zeros(bprev.shape, F32)
    for b in range(REL_BUCKETS):
        val = rb_ref[h, b] * LOG2E
        town = jnp.where(bown == b, val, town)
        tprev = jnp.where(bprev == b, val, tprev)
    town_ref[...] = town
    tprev_ref[...] = tprev


def _bias_tiles(rel_bias_t):
    r = np.arange(MOBA_BLOCK)
    rel_own = r[None, :] - r[:, None]
    bown = np.where(rel_own >= 0, _t5_bucket_np(rel_own), -1).astype(np.int32)
    bprev = _t5_bucket_np(MOBA_BLOCK + rel_own)
    tile = pl.BlockSpec((MOBA_BLOCK, MOBA_BLOCK), lambda h: (0, 0))
    out_tile = pl.BlockSpec((None, MOBA_BLOCK, MOBA_BLOCK), lambda h: (h, 0, 0))
    shape = jax.ShapeDtypeStruct((ATTN_HEADS, MOBA_BLOCK, MOBA_BLOCK), F32)
    return pl.pallas_call(
        _bias_tiles_kernel,
        grid=(ATTN_HEADS,),
        in_specs=[pl.BlockSpec(memory_space=pltpu.SMEM), tile, tile],
        out_specs=[out_tile, out_tile],
        out_shape=[shape, shape],
        compiler_params=_params("parallel"),
        name="t5_bias_tiles",
    )(rel_bias_t, jnp.asarray(bown), jnp.asarray(bprev))


KM_ROWS = 2 * SUBLANES
MOBA_KEY_TILE = 256


def _moba_kernel(rb_ref, q_ref, k_ref, v_ref, town_ref, tprev_ref, o_ref, vt_ref, km_ref, s_ref):
    h = pl.program_id(1)
    nb = k_ref.shape[0] // MOBA_BLOCK
    scale = ATTN_HEAD_DIM ** -0.5 * LOG2E
    far_bias = rb_ref[h, REL_BUCKETS - 1] * LOG2E
    blocks = [slice(jb * MOBA_BLOCK, (jb + 1) * MOBA_BLOCK) for jb in range(nb)]
    tiles_per_block = MOBA_BLOCK // MOBA_KEY_TILE

    km_ref[...] = jnp.zeros(km_ref.shape, F32)
    for jb in range(nb):
        vt_ref[:, blocks[jb]] = v_ref[blocks[jb], :].astype(F32).T.astype(BF16)
        km_ref[jb:jb + 1, :] = jnp.mean(k_ref[blocks[jb], :].astype(F32), axis=0, keepdims=True)
    km = km_ref[...]
    km_hi = km.astype(BF16)
    km_lo = (km - km_hi.astype(F32)).astype(BF16)

    for i in range(nb):
        q = q_ref[blocks[i], :]
        gated = i > MOBA_TOPK
        if gated:
            gate_t = (_dot_nt(km_hi, q) + _dot_nt(km_lo, q))[0:SUBLANES, :]
            blk = lax.broadcasted_iota(jnp.int32, gate_t.shape, 0)
            rank = jnp.zeros(gate_t.shape, jnp.int32)
            for jp in range(i):
                row = gate_t[jp:jp + 1, :]
                beats = (row > gate_t) | ((row == gate_t) & (jp < blk))
                rank = rank + jnp.where(beats, 1, 0)
            mneg_t = jnp.where((rank < MOBA_TOPK) & (blk < i), 0.0, NEG)

        buf = i % 2
        key_tiles = [(j, kt) for j in range(i + 1) for kt in range(tiles_per_block)]
        m = None
        for n, (j, kt) in enumerate(key_tiles):
            in_blk = slice(kt * MOBA_KEY_TILE, (kt + 1) * MOBA_KEY_TILE)
            keys = slice(j * MOBA_BLOCK + in_blk.start, j * MOBA_BLOCK + in_blk.stop)
            s = _dot_nt(k_ref[keys, :], q) * scale
            if j == i:
                s = s + town_ref[in_blk, :]
            else:
                mrow = mneg_t[j:j + 1, :] if gated else None
                if j == i - 1:
                    s = s + tprev_ref[in_blk, :]
                    if gated:
                        s = s + mrow
                else:
                    s = s + (far_bias + mrow if gated else far_bias)
            s_ref[buf, n] = s
            cmax = jnp.max(s, axis=0, keepdims=True)
            m = cmax if m is None else jnp.maximum(m, cmax)

        l = jnp.zeros((1, MOBA_BLOCK), F32)
        acc = jnp.zeros((ATTN_HEAD_DIM, MOBA_BLOCK), F32)
        for n, (j, kt) in enumerate(key_tiles):
            keys = slice(j * MOBA_BLOCK + kt * MOBA_KEY_TILE, j * MOBA_BLOCK + (kt + 1) * MOBA_KEY_TILE)
            p = jnp.exp2(s_ref[buf, n] - m)
            l = l + jnp.sum(p, axis=0, keepdims=True)
            acc = acc + _dot(vt_ref[:, keys], p.astype(BF16))
        o_ref[blocks[i], :] = (acc / l).T.astype(o_ref.dtype)


def _moba(qkv, rel_bias_t, town, tprev, bsz, seq):
    nb = seq // MOBA_BLOCK
    assert (MOBA_BLOCK + MOBA_BLOCK) > REL_MAX_DIST and nb <= SUBLANES
    tile = pl.BlockSpec((None, MOBA_BLOCK, MOBA_BLOCK), lambda b, h: (h, 0, 0))
    head_cols = lambda part: pl.BlockSpec((None, seq, ATTN_HEAD_DIM), lambda b, h: (b, 0, part * ATTN_HEADS + h))
    return pl.pallas_call(
        _moba_kernel,
        grid=(bsz, ATTN_HEADS),
        in_specs=[pl.BlockSpec(memory_space=pltpu.SMEM), head_cols(0), head_cols(1), head_cols(2), tile, tile],
        out_specs=pl.BlockSpec((None, seq, ATTN_HEAD_DIM), lambda b, h: (b, 0, h)),
        out_shape=jax.ShapeDtypeStruct((bsz, seq, ATTN_WIDTH), BF16),
        scratch_shapes=[pltpu.VMEM((ATTN_HEAD_DIM, seq), BF16),
                        pltpu.VMEM((KM_ROWS, ATTN_HEAD_DIM), F32),
                        pltpu.VMEM((2, seq // MOBA_KEY_TILE, MOBA_KEY_TILE, MOBA_BLOCK), F32)],
        compiler_params=_params("parallel", "parallel"),
        name="moba_attention",
    )(rel_bias_t, qkv, qkv, qkv, town, tprev)


def _silu(x):
    return x * jax.nn.sigmoid(x)


def _ssd_kernel(z_ref, xs_ref, bc_ref, dt_ref, cwx_ref, cwbc_ref, cbx_ref, cbbc_ref, dtb_ref, alog_ref,
                dskip_ref, normw_ref, expand_ref, ltri_ref, o_ref, xbuf_ref, bcbuf_ref, st_ref, y_ref):
    c = pl.program_id(1)
    L = SSM_CHUNK
    H = CONV_HALO

    @pl.when(c == 0)
    def _():
        xbuf_ref[0:H, :] = jnp.zeros((H, xbuf_ref.shape[1]), F32)
        bcbuf_ref[0:H, :] = jnp.zeros((H, bcbuf_ref.shape[1]), F32)
        st_ref[...] = jnp.zeros(st_ref.shape, F32)

    xbuf_ref[H:H + L, :] = xs_ref[...]
    bcbuf_ref[H:H + L, :] = bc_ref[...]

    def conv_silu(buf_ref, w_ref, b_ref):
        u = buf_ref[...]
        acc = w_ref[0:1, :] * u
        for k in range(1, SSM_CONV):
            acc = pltpu.roll(acc, 1, 0) + w_ref[k:k + 1, :] * u
        return _silu(acc[H:, :] + b_ref[...])

    xs = conv_silu(xbuf_ref, cwx_ref, cbx_ref)
    bc = conv_silu(bcbuf_ref, cwbc_ref, cbbc_ref)
    xbuf_ref[0:H, :] = xbuf_ref[L:L + H, :]
    bcbuf_ref[0:H, :] = bcbuf_ref[L:L + H, :]

    x_dt = dt_ref[...] + dtb_ref[...]
    dt = jnp.maximum(x_dt, 0.0) + jnp.log1p(jnp.exp(-jnp.abs(x_dt)))
    da = dt * (-jnp.exp(alog_ref[...]) * LOG2E)
    ltri = ltri_ref[...]
    a_cs = sum(_dot(ltri, part) for part in _split3(da))
    a_last = a_cs[L - 1:L, :]
    decay_st = jnp.exp2(a_last - a_cs)
    exp_acs = jnp.exp2(a_cs)
    chunk_decay = jnp.broadcast_to(jnp.exp2(a_last), (2 * SUBLANES, LANES))

    stacked = jnp.concatenate([dt, decay_st, exp_acs, chunk_decay], axis=0)
    expand = expand_ref[...]
    wide = sum(_dot(part, expand) for part in _split3(stacked))
    dt_w = wide[0:L]
    decay_st_w = wide[L:2 * L]
    exp_acs_w = wide[2 * L:3 * L]
    chunk_decay_w = wide[3 * L:3 * L + 1]

    xc = xs * dt_w
    a_cs_t = a_cs.T
    row_i = lax.broadcasted_iota(jnp.int32, (L, L), 0)
    col_i = lax.broadcasted_iota(jnp.int32, (L, L), 1)
    causal = row_i >= col_i
    first_head = col_i < SSM_HEAD_DIM
    heads_per_group = SSM_HEADS // SSM_GROUPS

    for g in range(SSM_GROUPS):
        gsl = slice(g * SSM_GROUP_WIDTH, (g + 1) * SSM_GROUP_WIDTH)
        bg = bc[:, g * SSM_STATE:(g + 1) * SSM_STATE]
        cg = bc[:, (SSM_GROUPS + g) * SSM_STATE:(SSM_GROUPS + g + 1) * SSM_STATE].astype(BF16)
        cb = _dot_nt(cg, bg.astype(BF16))
        prev = st_ref[:, gsl]
        y_off = _dot(cg, prev.astype(BF16)) * exp_acs_w[:, gsl]
        st_ref[:, gsl] = prev * chunk_decay_w[:, gsl] + _dot(bg.T.astype(BF16),
                                                             (xc[:, gsl] * decay_st_w[:, gsl]).astype(BF16))
        for pr in range(heads_per_group // 2):
            h0 = g * heads_per_group + 2 * pr
            psl = slice(h0 * SSM_HEAD_DIM, (h0 + 2) * SSM_HEAD_DIM)
            xcp = xc[:, psl].astype(BF16)
            ys = []
            for hh in (h0, h0 + 1):
                seg = a_cs[:, hh:hh + 1] - a_cs_t[hh:hh + 1, :]
                decay = jnp.exp2(jnp.where(causal, seg, NEG))
                ys.append(_dot((cb * decay).astype(BF16), xcp))
            y_ref[:, psl] = jnp.where(first_head, ys[0], ys[1]) + y_off[:, 2 * pr * SSM_HEAD_DIM:
                                                                         (2 * pr + 2) * SSM_HEAD_DIM]

    y = (y_ref[...] + dskip_ref[...] * xs) * _silu(z_ref[...])
    normw = normw_ref[...]
    for g in range(SSM_GROUPS):
        gsl = slice(g * SSM_GROUP_WIDTH, (g + 1) * SSM_GROUP_WIDTH)
        yg = y[:, gsl]
        yg = yg * lax.rsqrt(jnp.mean(yg * yg, axis=-1, keepdims=True) + NORM_EPS)
        o_ref[:, gsl] = (yg * normw[:, gsl]).astype(o_ref.dtype)


def _ssd(rest, dt_raw, conv_w, conv_b, dt_bias, a_log, d_skip, norm_w, bsz, seq):
    L = SSM_CHUNK
    nc = seq // L
    pad_heads = LANES - SSM_HEADS
    expand = np.zeros((LANES, SSM_WIDTH), np.float32)
    for hh in range(SSM_HEADS):
        expand[hh, hh * SSM_HEAD_DIM:(hh + 1) * SSM_HEAD_DIM] = 1.0
    ltri = np.tril(np.ones((L, L), np.float32))

    row = lambda v: v.reshape(1, -1).astype(F32)
    const = lambda shape: pl.BlockSpec(shape, lambda b, c: (0, 0))
    args = (rest, rest, rest, dt_raw,
            conv_w[:, :SSM_WIDTH], conv_w[:, SSM_WIDTH:], row(conv_b[:SSM_WIDTH]), row(conv_b[SSM_WIDTH:]),
            row(jnp.pad(dt_bias, (0, pad_heads))), row(jnp.pad(a_log, (0, pad_heads))),
            row(jnp.repeat(d_skip, SSM_HEAD_DIM)), row(norm_w),
            jnp.asarray(expand, BF16), jnp.asarray(ltri, BF16))
    in_specs = [
        pl.BlockSpec((None, L, SSM_WIDTH), lambda b, c: (b, c, 0)),
        pl.BlockSpec((None, L, SSM_WIDTH), lambda b, c: (b, c, 1)),
        pl.BlockSpec((None, L, SSM_BC_WIDTH), lambda b, c: (b, c, 2 * SSM_WIDTH // SSM_BC_WIDTH)),
        pl.BlockSpec((None, L, LANES), lambda b, c: (b, c, 0)),
        const((SSM_CONV, SSM_WIDTH)), const((SSM_CONV, SSM_BC_WIDTH)),
        const((1, SSM_WIDTH)), const((1, SSM_BC_WIDTH)),
        const((1, LANES)), const((1, LANES)), const((1, SSM_WIDTH)), const((1, SSM_WIDTH)),
        const((LANES, SSM_WIDTH)), const((L, L)),
    ]
    return pl.pallas_call(
        _ssd_kernel,
        grid=(bsz, nc),
        in_specs=in_specs,
        out_specs=pl.BlockSpec((None, L, SSM_WIDTH), lambda b, c: (b, c, 0)),
        out_shape=jax.ShapeDtypeStruct((bsz, seq, SSM_WIDTH), BF16),
        scratch_shapes=[pltpu.VMEM((L + CONV_HALO, SSM_WIDTH), F32),
                        pltpu.VMEM((L + CONV_HALO, SSM_BC_WIDTH), F32),
                        pltpu.VMEM((SSM_STATE, SSM_WIDTH), F32),
                        pltpu.VMEM((L, SSM_WIDTH), F32)],
        compiler_params=_params("parallel", "arbitrary"),
        name="ssd_mixer",
    )(*args)


def _outproj_kernel(attn_ref, ssm_ref, w_ref, x_ref, gpost_ref, gnext_ref, xo_ref, ho_ref, wb_ref, *, sub):
    @pl.when(pl.program_id(0) == 0)
    def _():
        wb_ref[...] = w_ref[...].astype(BF16)

    for r0 in range(0, x_ref.shape[0], sub):
        rows = slice(r0, r0 + sub)
        mixed = _dot(attn_ref[rows, :], wb_ref[0:ATTN_WIDTH, :]) + _dot(ssm_ref[rows, :], wb_ref[ATTN_WIDTH:, :])
        x_new = x_ref[rows, :] + _rms(mixed, gpost_ref[...])
        xo_ref[rows, :] = x_new
        ho_ref[rows, :] = _rms(x_new, gnext_ref[...]).astype(ho_ref.dtype)


def _outproj(attn, ssm, w, layer, x, g_post, g_next, tm=512, sub=256):
    t, d = x.shape
    row = pl.BlockSpec((1, d), lambda m: (0, 0))
    return pl.pallas_call(
        functools.partial(_outproj_kernel, sub=sub),
        grid=(t // tm,),
        in_specs=[pl.BlockSpec((tm, ATTN_WIDTH), lambda m: (m, 0)),
                  pl.BlockSpec((tm, SSM_WIDTH), lambda m: (m, 0)),
                  pl.BlockSpec((None,) + w.shape[1:], lambda m: (layer, 0, 0), pipeline_mode=pl.Buffered(1)),
                  pl.BlockSpec((tm, d), lambda m: (m, 0)),
                  row, row],
        out_specs=[pl.BlockSpec((tm, d), lambda m: (m, 0)), pl.BlockSpec((tm, d), lambda m: (m, 0))],
        out_shape=[jax.ShapeDtypeStruct((t, d), F32), jax.ShapeDtypeStruct((t, d), BF16)],
        scratch_shapes=[pltpu.VMEM(w.shape[1:], BF16)],
        compiler_params=_params("arbitrary"),
        name="out_proj",
    )(attn, ssm, w, x, g_post.reshape(1, d), g_next.reshape(1, d))


def _ffn_up_kernel(h_ref, wg_ref, wu_ref, cwg_ref, cwu_ref, cbg_ref, cbu_ref, o_ref, wgb_ref, wub_ref,
                   gbuf_ref, ubuf_ref, *, row_tiles):
    assert sum(row_tiles) == h_ref.shape[0]
    H = CONV_HALO

    @pl.when(pl.program_id(1) == 0)
    def _():
        wgb_ref[...] = wg_ref[...].astype(BF16)
        wub_ref[...] = wu_ref[...].astype(BF16)
        gbuf_ref[0:H, :] = jnp.zeros((H, gbuf_ref.shape[1]), F32)
        ubuf_ref[0:H, :] = jnp.zeros((H, ubuf_ref.shape[1]), F32)

    def conv(buf_ref, cw_ref, cb_ref, r0, tm):
        u = buf_ref[r0:r0 + H + tm, :]
        acc = cw_ref[0:1, :] * u
        for k in range(1, FFN_CONV):
            acc = pltpu.roll(acc, 1, 0) + cw_ref[k:k + 1, :] * u
        return acc[H:, :] + cb_ref[...]

    r0 = 0
    for tm in row_tiles:
        hmat = h_ref[r0:r0 + tm, :]
        gbuf_ref[H + r0:H + r0 + tm, :] = _dot(hmat, wgb_ref[...])
        ubuf_ref[H + r0:H + r0 + tm, :] = _dot(hmat, wub_ref[...])
        gate = conv(gbuf_ref, cwg_ref, cbg_ref, r0, tm)
        up = conv(ubuf_ref, cwu_ref, cbu_ref, r0, tm)
        o_ref[r0:r0 + tm, :] = (jax.nn.gelu(gate, approximate=True) * up).astype(o_ref.dtype)
        r0 += tm


def _ffn_up(h, w_up, conv_w, conv_b, layer, seq, row_tiles=(1024, 1024), tn=512):
    t, d = h.shape
    nt = FFN_HIDDEN // tn
    col = lambda rows, half: pl.BlockSpec((None, rows, tn), lambda j, b: (layer, 0, half * nt + j))
    return pl.pallas_call(
        functools.partial(_ffn_up_kernel, row_tiles=row_tiles),
        grid=(nt, t // seq),
        in_specs=[pl.BlockSpec((seq, d), lambda j, b: (b, 0)),
                  col(d, 0), col(d, 1), col(FFN_CONV, 0), col(FFN_CONV, 1), col(1, 0), col(1, 1)],
        out_specs=pl.BlockSpec((seq, tn), lambda j, b: (b, j)),
        out_shape=jax.ShapeDtypeStruct((t, FFN_HIDDEN), BF16),
        scratch_shapes=[pltpu.VMEM((d, tn), BF16), pltpu.VMEM((d, tn), BF16),
                        pltpu.VMEM((seq + CONV_HALO, tn), F32), pltpu.VMEM((seq + CONV_HALO, tn), F32)],
        compiler_params=_params("parallel", "arbitrary"),
        name="ffn_up_conv_geglu",
    )(h, w_up, w_up, conv_w, conv_w, conv_b, conv_b)


def _ffn_down_kernel(a_ref, w_ref, x_ref, gpost_ref, gnext_ref, xo_ref, ho_ref, acc_ref):
    k = pl.program_id(1)

    @pl.when(k == 0)
    def _():
        acc_ref[...] = jnp.zeros(acc_ref.shape, F32)

    acc_ref[...] += _dot(a_ref[...], w_ref[...])

    @pl.when(k == pl.num_programs(1) - 1)
    def _():
        x_new = x_ref[...] + _rms(acc_ref[...], gpost_ref[...])
        xo_ref[...] = x_new
        ho_ref[...] = _rms(x_new, gnext_ref[...]).astype(ho_ref.dtype)


def _ffn_down(act, w, layer, x, g_post, g_next, tm=512, tk=2816):
    t, d = x.shape
    kdim = act.shape[1]
    row = pl.BlockSpec((1, d), lambda m, k: (0, 0))
    return pl.pallas_call(
        _ffn_down_kernel,
        grid=(t // tm, kdim // tk),
        in_specs=[pl.BlockSpec((tm, tk), lambda m, k: (m, k)),
                  pl.BlockSpec((None, tk, d), lambda m, k: (layer, k, 0)),
                  pl.BlockSpec((tm, d), lambda m, k: (m, 0)),
                  row, row],
        out_specs=[pl.BlockSpec((tm, d), lambda m, k: (m, 0)), pl.BlockSpec((tm, d), lambda m, k: (m, 0))],
        out_shape=[jax.ShapeDtypeStruct((t, d), F32), jax.ShapeDtypeStruct((t, d), BF16)],
        scratch_shapes=[pltpu.VMEM((tm, d), F32)],
        compiler_params=_params("parallel", "arbitrary"),
        name="ffn_down",
    )(act, w, x, g_post.reshape(1, d), g_next.reshape(1, d))


def kernel(x, rel_bias, ln_mix_pre, w_in, ssm_conv_w, ssm_conv_b, dt_bias, a_log, d_skip, ssm_norm_w, w_out,
           ln_mix_post, ln_ffn_pre, w_ffn_up, ffn_conv_w, ffn_conv_b, w_ffn_down, ln_ffn_post):
    bsz, seq, d = x.shape
    depth = w_in.shape[0]
    t = bsz * seq
    assert w_in.shape[2] == QKV_COLS + REST_COLS + SSM_HEADS

    rel_bias_t = rel_bias.T.astype(F32)
    town, tprev = _bias_tiles(rel_bias_t)
    w_in_t = jnp.swapaxes(w_in, 1, 2)
    w_dt_t = jnp.pad(w_in_t[:, QKV_COLS + REST_COLS:, :], ((0, 0), (0, LANES - SSM_HEADS), (0, 0)))
    ffn_conv_b3 = ffn_conv_b.reshape(depth, 1, -1)
    w_down_bf16 = w_ffn_down.astype(BF16)
    qkv_tn, rest_tn = 1024, 512

    xf = x.reshape(t, d)
    h = _rmsnorm(xf, ln_mix_pre[0])
    for l in range(depth):
        qkv = _matmul(h, w_in_t, l, 0, QKV_COLS // qkv_tn, BF16, tm=1024, tn=qkv_tn, name="in_proj_qkv")
        rest = _matmul(h, w_in_t, l, QKV_COLS // rest_tn, REST_COLS // rest_tn, F32, tm=1024, tn=rest_tn,
                       name="in_proj_ssm")
        dt_raw = _matmul(h, w_dt_t, l, 0, 1, F32, tm=1024, tn=LANES, name="in_proj_dt")
        attn = _moba(qkv.reshape(bsz, seq, QKV_COLS), rel_bias_t, town, tprev, bsz, seq)
        ssm = _ssd(rest.reshape(bsz, seq, REST_COLS), dt_raw.reshape(bsz, seq, LANES), ssm_conv_w[l],
                   ssm_conv_b[l], dt_bias[l], a_log[l], d_skip[l], ssm_norm_w[l], bsz, seq)
        xf, h = _outproj(attn.reshape(t, ATTN_WIDTH), ssm.reshape(t, SSM_WIDTH), w_out, l, xf,
                         ln_mix_post[l], ln_ffn_pre[l])
        act = _ffn_up(h, w_ffn_up, ffn_conv_w, ffn_conv_b3, l, seq)
        g_next = ln_mix_pre[(l + 1) % depth]
        xf, h = _ffn_down(act, w_down_bf16, l, xf, ln_ffn_post[l], g_next)
    return xf.reshape(bsz, seq, d)
```

```python
import functools
import math

import jax
import jax.numpy as jnp
import numpy as np
from jax import lax
from jax.experimental import pallas as pl
from jax.experimental.pallas import tpu as pltpu

F32 = jnp.float32
BF16 = jnp.bfloat16

LANES = 128
SUBLANES = 8
VMEM_LIMIT_BYTES = 56 * 1024 * 1024

D_MODEL = 2048
ATTN_HEAD_DIM = 128
ATTN_WIDTH = 1024
ATTN_HEADS = ATTN_WIDTH // ATTN_HEAD_DIM
MOBA_BLOCK = 256
MOBA_TOPK = 3
REL_BUCKETS = 32
REL_MAX_DIST = 128
SSM_WIDTH = 1024
SSM_HEAD_DIM = 64
SSM_HEADS = SSM_WIDTH // SSM_HEAD_DIM
SSM_GROUPS = 2
SSM_GROUP_WIDTH = SSM_WIDTH // SSM_GROUPS
SSM_STATE = 128
SSM_CONV = 4
SSM_CHUNK = 128
SSM_BC_WIDTH = 2 * SSM_GROUPS * SSM_STATE
FFN_HIDDEN = 5632
FFN_CONV = 3
NORM_EPS = 1e-6
NEG = -1e30
LOG2E = math.log2(math.e)

QKV_COLS = 3 * ATTN_WIDTH
REST_COLS = SSM_WIDTH + SSM_WIDTH + SSM_BC_WIDTH
CONV_HALO = SUBLANES


def _params(*semantics):
    return pltpu.CompilerParams(dimension_semantics=semantics, vmem_limit_bytes=VMEM_LIMIT_BYTES)


def _rms(x, w):
    return x * lax.rsqrt(jnp.mean(x * x, axis=-1, keepdims=True) + NORM_EPS) * w


def _split3(x):
    hi = x.astype(BF16)
    r1 = x - hi.astype(F32)
    mid = r1.astype(BF16)
    lo = (r1 - mid.astype(F32)).astype(BF16)
    return hi, mid, lo


def _dot(a, b):
    return jnp.dot(a, b, preferred_element_type=F32)


def _dot_nt(a, b):
    return lax.dot_general(a, b, (((1,), (1,)), ((), ())), preferred_element_type=F32)


def _rmsnorm_kernel(x_ref, w_ref, o_ref):
    o_ref[...] = _rms(x_ref[...], w_ref[...]).astype(o_ref.dtype)


def _rmsnorm(x, w, tm=512):
    t, d = x.shape
    return pl.pallas_call(
        _rmsnorm_kernel,
        grid=(t // tm,),
        in_specs=[pl.BlockSpec((tm, d), lambda m: (m, 0)),
                  pl.BlockSpec((1, d), lambda m: (0, 0))],
        out_specs=pl.BlockSpec((tm, d), lambda m: (m, 0)),
        out_shape=jax.ShapeDtypeStruct((t, d), BF16),
        compiler_params=_params("parallel"),
        name="rmsnorm",
    )(x, w.reshape(1, d))


IN_TN = 512
IN_QKV_TILES = QKV_COLS // IN_TN
IN_REST_TILES = REST_COLS // IN_TN


def _inproj_kernel(h_ref, wt_ref, wdt_ref, qkv_ref, rest_ref, dt_ref, wb_ref, wdtb_ref):
    m = pl.program_id(0)
    j = pl.program_id(1)

    @pl.when(m == 0)
    def _():
        wb_ref[j] = wt_ref[...].T.astype(BF16)

    @pl.when((m == 0) & (j == 0))
    def _():
        wdtb_ref[...] = wdt_ref[...].T.astype(BF16)

    @pl.when(j < IN_QKV_TILES)
    def _():
        qkv_ref[...] = _dot(h_ref[...], wb_ref[j]).astype(qkv_ref.dtype)

    @pl.when(j >= IN_QKV_TILES)
    def _():
        rest_ref[...] = _dot(h_ref[...], wb_ref[j])

    @pl.when(j == 0)
    def _():
        dt_ref[...] = _dot(h_ref[...], wdtb_ref[...])


def _inproj(h, w_in_t, w_dt_t, layer, tm=1024):
    t, d = h.shape
    n_tiles = IN_QKV_TILES + IN_REST_TILES
    w_index = lambda m, j: (layer, jnp.where(m == 0, j, n_tiles - 1), 0)
    return pl.pallas_call(
        _inproj_kernel,
        grid=(t // tm, n_tiles),
        in_specs=[pl.BlockSpec((tm, d), lambda m, j: (m, 0)),
                  pl.BlockSpec((None, IN_TN, d), w_index),
                  pl.BlockSpec((None, LANES, d), lambda m, j: (layer, 0, 0))],
        out_specs=[pl.BlockSpec((tm, IN_TN), lambda m, j: (m, jnp.minimum(j, IN_QKV_TILES - 1))),
                   pl.BlockSpec((tm, IN_TN), lambda m, j: (m, jnp.maximum(j - IN_QKV_TILES, 0))),
                   pl.BlockSpec((tm, LANES), lambda m, j: (m, 0))],
        out_shape=[jax.ShapeDtypeStruct((t, QKV_COLS), BF16),
                   jax.ShapeDtypeStruct((t, REST_COLS), F32),
                   jax.ShapeDtypeStruct((t, LANES), F32)],
        scratch_shapes=[pltpu.VMEM((n_tiles, d, IN_TN), BF16), pltpu.VMEM((d, LANES), BF16)],
        compiler_params=_params("arbitrary", "arbitrary"),
        name="in_proj",
    )(h, w_in_t, w_dt_t)


def _t5_bucket_np(rel):
    n = np.maximum(rel, 0)
    max_exact = REL_BUCKETS // 2
    nf = np.maximum(n, 1).astype(np.float32)
    large = max_exact + (np.log(nf / np.float32(max_exact)) / np.float32(math.log(REL_MAX_DIST / max_exact))
                         * np.float32(REL_BUCKETS - max_exact)).astype(np.int32)
    large = np.minimum(large, REL_BUCKETS - 1)
    return np.where(n < max_exact, n, large).astype(np.int32)


def _bias_tiles_kernel(rb_ref, bown_ref, bprev_ref, town_ref, tprev_ref):
    h = pl.program_id(0)
    bown = bown_ref[...]
    bprev = bprev_ref[...]
    town = jnp.full(bown.shape, NEG, F32)
    tprev = jnp.zeros(bprev.shape, F32)
    for b in range(REL_BUCKETS):
        val = rb_ref[h, b] * LOG2E
        town = jnp.where(bown == b, val, town)
        tprev = jnp.where(bprev == b, val, tprev)
    town_ref[...] = town
    tprev_ref[...] = tprev


def _bias_tiles(rel_bias_t):
    r = np.arange(MOBA_BLOCK)
    rel_own = r[None, :] - r[:, None]
    bown = np.where(rel_own >= 0, _t5_bucket_np(rel_own), -1).astype(np.int32)
    bprev = _t5_bucket_np(MOBA_BLOCK + rel_own)
    tile = pl.BlockSpec((MOBA_BLOCK, MOBA_BLOCK), lambda h: (0, 0))
    out_tile = pl.BlockSpec((None, MOBA_BLOCK, MOBA_BLOCK), lambda h: (h, 0, 0))
    shape = jax.ShapeDtypeStruct((ATTN_HEADS, MOBA_BLOCK, MOBA_BLOCK), F32)
    return pl.pallas_call(
        _bias_tiles_kernel,
        grid=(ATTN_HEADS,),
        in_specs=[pl.BlockSpec(memory_space=pltpu.SMEM), tile, tile],
        out_specs=[out_tile, out_tile],
        out_shape=[shape, shape],
        compiler_params=_params("parallel"),
        name="t5_bias_tiles",
    )(rel_bias_t, jnp.asarray(bown), jnp.asarray(bprev))


KM_ROWS = 2 * SUBLANES
MOBA_KEY_TILE = 256


def _moba_kernel(rb_ref, q_ref, k_ref, v_ref, town_ref, tprev_ref, o_ref, vt_ref, km_ref, s_ref):
    h = pl.program_id(1)
    nb = k_ref.shape[0] // MOBA_BLOCK
    scale = ATTN_HEAD_DIM ** -0.5 * LOG2E
    far_bias = rb_ref[h, REL_BUCKETS - 1] * LOG2E
    blocks = [slice(jb * MOBA_BLOCK, (jb + 1) * MOBA_BLOCK) for jb in range(nb)]
    tiles_per_block = MOBA_BLOCK // MOBA_KEY_TILE

    km_ref[...] = jnp.zeros(km_ref.shape, F32)
    for jb in range(nb):
        vt_ref[:, blocks[jb]] = v_ref[blocks[jb], :].astype(F32).T.astype(BF16)
        km_ref[jb:jb + 1, :] = jnp.mean(k_ref[blocks[jb], :].astype(F32), axis=0, keepdims=True)
    km = km_ref[...]
    km_hi = km.astype(BF16)
    km_lo = (km - km_hi.astype(F32)).astype(BF16)

    for i in range(nb):
        q = q_ref[blocks[i], :]
        gated = i > MOBA_TOPK
        if gated:
            gate_t = (_dot_nt(km_hi, q) + _dot_nt(km_lo, q))[0:SUBLANES, :]
            blk = lax.broadcasted_iota(jnp.int32, gate_t.shape, 0)
            rank = jnp.zeros(gate_t.shape, jnp.int32)
            for jp in range(i):
                row = gate_t[jp:jp + 1, :]
                beats = (row > gate_t) | ((row == gate_t) & (jp < blk))
                rank = rank + jnp.where(beats, 1, 0)
            mneg_t = jnp.where((rank < MOBA_TOPK) & (blk < i), 0.0, NEG)

        buf = i % 2
        key_tiles = [(j, kt) for j in range(i + 1) for kt in range(tiles_per_block)]
        m = None
        for n, (j, kt) in enumerate(key_tiles):
            in_blk = slice(kt * MOBA_KEY_TILE, (kt + 1) * MOBA_KEY_TILE)
            keys = slice(j * MOBA_BLOCK + in_blk.start, j * MOBA_BLOCK + in_blk.stop)
            s = _dot_nt(k_ref[keys, :], q) * scale
            if j == i:
                s = s + town_ref[in_blk, :]
            else:
                mrow = mneg_t[j:j + 1, :] if gated else None
                if j == i - 1:
                    s = s + tprev_ref[in_blk, :]
                    if gated:
                        s = s + mrow
                else:
                    s = s + (far_bias + mrow if gated else far_bias)
            s_ref[buf, n] = s
            cmax = jnp.max(s, axis=0, keepdims=True)
            m = cmax if m is None else jnp.maximum(m, cmax)

        l = jnp.zeros((1, MOBA_BLOCK), F32)
        acc = jnp.zeros((ATTN_HEAD_DIM, MOBA_BLOCK), F32)
        for n, (j, kt) in enumerate(key_tiles):
            keys = slice(j * MOBA_BLOCK + kt * MOBA_KEY_TILE, j * MOBA_BLOCK + (kt + 1) * MOBA_KEY_TILE)
            p = jnp.exp2(s_ref[buf, n] - m)
            l = l + jnp.sum(p, axis=0, keepdims=True)
            acc = acc + _dot(vt_ref[:, keys], p.astype(BF16))
        o_ref[blocks[i], :] = (acc / l).T.astype(o_ref.dtype)


def _moba(qkv, rel_bias_t, town, tprev, bsz, seq):
    nb = seq // MOBA_BLOCK
    assert (MOBA_BLOCK + MOBA_BLOCK) > REL_MAX_DIST and nb <= SUBLANES
    tile = pl.BlockSpec((None, MOBA_BLOCK, MOBA_BLOCK), lambda b, h: (h, 0, 0))
    head_cols = lambda part: pl.BlockSpec((None, seq, ATTN_HEAD_DIM), lambda b, h: (b, 0, part * ATTN_HEADS + h))
    return pl.pallas_call(
        _moba_kernel,
        grid=(bsz, ATTN_HEADS),
        in_specs=[pl.BlockSpec(memory_space=pltpu.SMEM), head_cols(0), head_cols(1), head_cols(2), tile, tile],
        out_specs=pl.BlockSpec((None, seq, ATTN_HEAD_DIM), lambda b, h: (b, 0, h)),
        out_shape=jax.ShapeDtypeStruct((bsz, seq, ATTN_WIDTH), BF16),
        scratch_shapes=[pltpu.VMEM((ATTN_HEAD_DIM, seq), BF16),
                        pltpu.VMEM((KM_ROWS, ATTN_HEAD_DIM), F32),
                        pltpu.VMEM((2, seq // MOBA_KEY_TILE, MOBA_KEY_TILE, MOBA_BLOCK), F32)],
        compiler_params=_params("parallel", "parallel"),
        name="moba_attention",
    )(rel_bias_t, qkv, qkv, qkv, town, tprev)


def _silu(x):
    return x * jax.nn.sigmoid(x)


def _ssd_kernel(z_ref, xs_ref, bc_ref, dt_ref, cwx_ref, cwbc_ref, cbx_ref, cbbc_ref, dtb_ref, alog_ref,
                dskip_ref, normw_ref, expand_ref, ltri_ref, o_ref, xbuf_ref, bcbuf_ref, st_ref, y_ref):
    c = pl.program_id(1)
    L = SSM_CHUNK
    H = CONV_HALO

    @pl.when(c == 0)
    def _():
        xbuf_ref[0:H, :] = jnp.zeros((H, xbuf_ref.shape[1]), F32)
        bcbuf_ref[0:H, :] = jnp.zeros((H, bcbuf_ref.shape[1]), F32)
        st_ref[...] = jnp.zeros(st_ref.shape, F32)

    xbuf_ref[H:H + L, :] = xs_ref[...]
    bcbuf_ref[H:H + L, :] = bc_ref[...]

    def conv_silu(buf_ref, w_ref, b_ref):
        u = buf_ref[...]
        acc = w_ref[0:1, :] * u
        for k in range(1, SSM_CONV):
            acc = pltpu.roll(acc, 1, 0) + w_ref[k:k + 1, :] * u
        return _silu(acc[H:, :] + b_ref[...])

    xs = conv_silu(xbuf_ref, cwx_ref, cbx_ref)
    bc = conv_silu(bcbuf_ref, cwbc_ref, cbbc_ref)
    xbuf_ref[0:H, :] = xbuf_ref[L:L + H, :]
    bcbuf_ref[0:H, :] = bcbuf_ref[L:L + H, :]

    x_dt = dt_ref[...] + dtb_ref[...]
    dt = jnp.maximum(x_dt, 0.0) + jnp.log1p(jnp.exp(-jnp.abs(x_dt)))
    da = dt * (-jnp.exp(alog_ref[...]) * LOG2E)
    ltri = ltri_ref[...]
    a_cs = sum(_dot(ltri, part) for part in _split3(da))
    a_last = a_cs[L - 1:L, :]
    decay_st = jnp.exp2(a_last - a_cs)
    exp_acs = jnp.exp2(a_cs)
    chunk_decay = jnp.broadcast_to(jnp.exp2(a_last), (2 * SUBLANES, LANES))

    stacked = jnp.concatenate([dt, decay_st, exp_acs, chunk_decay], axis=0)
    expand = expand_ref[...]
    wide = sum(_dot(part, expand) for part in _split3(stacked))
    dt_w = wide[0:L]
    decay_st_w = wide[L:2 * L]
    exp_acs_w = wide[2 * L:3 * L]
    chunk_decay_w = wide[3 * L:3 * L + 1]

    xc = xs * dt_w
    a_cs_t = a_cs.T
    row_i = lax.broadcasted_iota(jnp.int32, (L, L), 0)
    col_i = lax.broadcasted_iota(jnp.int32, (L, L), 1)
    causal = row_i >= col_i
    first_head = col_i < SSM_HEAD_DIM
    heads_per_group = SSM_HEADS // SSM_GROUPS

    for g in range(SSM_GROUPS):
        gsl = slice(g * SSM_GROUP_WIDTH, (g + 1) * SSM_GROUP_WIDTH)
        bg = bc[:, g * SSM_STATE:(g + 1) * SSM_STATE]
        cg = bc[:, (SSM_GROUPS + g) * SSM_STATE:(SSM_GROUPS + g + 1) * SSM_STATE].astype(BF16)
        cb = _dot_nt(cg, bg.astype(BF16))
        prev = st_ref[:, gsl]
        y_off = _dot(cg, prev.astype(BF16)) * exp_acs_w[:, gsl]
        st_ref[:, gsl] = prev * chunk_decay_w[:, gsl] + _dot(bg.T.astype(BF16),
                                                             (xc[:, gsl] * decay_st_w[:, gsl]).astype(BF16))
        for pr in range(heads_per_group // 2):
            h0 = g * heads_per_group + 2 * pr
            psl = slice(h0 * SSM_HEAD_DIM, (h0 + 2) * SSM_HEAD_DIM)
            xcp = xc[:, psl].astype(BF16)
            ys = []
            for hh in (h0, h0 + 1):
                seg = a_cs[:, hh:hh + 1] - a_cs_t[hh:hh + 1, :]
                decay = jnp.exp2(jnp.where(causal, seg, NEG))
                ys.append(_dot((cb * decay).astype(BF16), xcp))
            y_ref[:, psl] = jnp.where(first_head, ys[0], ys[1]) + y_off[:, 2 * pr * SSM_HEAD_DIM:
                                                                         (2 * pr + 2) * SSM_HEAD_DIM]

    y = (y_ref[...] + dskip_ref[...] * xs) * _silu(z_ref[...])
    normw = normw_ref[...]
    for g in range(SSM_GROUPS):
        gsl = slice(g * SSM_GROUP_WIDTH, (g + 1) * SSM_GROUP_WIDTH)
        yg = y[:, gsl]
        yg = yg * lax.rsqrt(jnp.mean(yg * yg, axis=-1, keepdims=True) + NORM_EPS)
        o_ref[:, gsl] = (yg * normw[:, gsl]).astype(o_ref.dtype)


def _ssd(rest, dt_raw, conv_w, conv_b, dt_bias, a_log, d_skip, norm_w, bsz, seq):
    L = SSM_CHUNK
    nc = seq // L
    pad_heads = LANES - SSM_HEADS
    expand = np.zeros((LANES, SSM_WIDTH), np.float32)
    for hh in range(SSM_HEADS):
        expand[hh, hh * SSM_HEAD_DIM:(hh + 1) * SSM_HEAD_DIM] = 1.0
    ltri = np.tril(np.ones((L, L), np.float32))

    row = lambda v: v.reshape(1, -1).astype(F32)
    const = lambda shape: pl.BlockSpec(shape, lambda b, c: (0, 0))
    args = (rest, rest, rest, dt_raw,
            conv_w[:, :SSM_WIDTH], conv_w[:, SSM_WIDTH:], row(conv_b[:SSM_WIDTH]), row(conv_b[SSM_WIDTH:]),
            row(jnp.pad(dt_bias, (0, pad_heads))), row(jnp.pad(a_log, (0, pad_heads))),
            row(jnp.repeat(d_skip, SSM_HEAD_DIM)), row(norm_w),
            jnp.asarray(expand, BF16), jnp.asarray(ltri, BF16))
    in_specs = [
        pl.BlockSpec((None, L, SSM_WIDTH), lambda b, c: (b, c, 0)),
        pl.BlockSpec((None, L, SSM_WIDTH), lambda b, c: (b, c, 1)),
        pl.BlockSpec((None, L, SSM_BC_WIDTH), lambda b, c: (b, c, 2 * SSM_WIDTH // SSM_BC_WIDTH)),
        pl.BlockSpec((None, L, LANES), lambda b, c: (b, c, 0)),
        const((SSM_CONV, SSM_WIDTH)), const((SSM_CONV, SSM_BC_WIDTH)),
        const((1, SSM_WIDTH)), const((1, SSM_BC_WIDTH)),
        const((1, LANES)), const((1, LANES)), const((1, SSM_WIDTH)), const((1, SSM_WIDTH)),
        const((LANES, SSM_WIDTH)), const((L, L)),
    ]
    return pl.pallas_call(
        _ssd_kernel,
        grid=(bsz, nc),
        in_specs=in_specs,
        out_specs=pl.BlockSpec((None, L, SSM_WIDTH), lambda b, c: (b, c, 0)),
        out_shape=jax.ShapeDtypeStruct((bsz, seq, SSM_WIDTH), BF16),
        scratch_shapes=[pltpu.VMEM((L + CONV_HALO, SSM_WIDTH), F32),
                        pltpu.VMEM((L + CONV_HALO, SSM_BC_WIDTH), F32),
                        pltpu.VMEM((SSM_STATE, SSM_WIDTH), F32),
                        pltpu.VMEM((L, SSM_WIDTH), F32)],
        compiler_params=_params("parallel", "arbitrary"),
        name="ssd_mixer",
    )(*args)


def _outproj_kernel(attn_ref, ssm_ref, w_ref, x_ref, gpost_ref, gnext_ref, xo_ref, ho_ref, wb_ref, *, sub):
    @pl.when(pl.program_id(0) == 0)
    def _():
        wb_ref[...] = w_ref[...].astype(BF16)

    for r0 in range(0, x_ref.shape[0], sub):
        rows = slice(r0, r0 + sub)
        mixed = _dot(attn_ref[rows, :], wb_ref[0:ATTN_WIDTH, :]) + _dot(ssm_ref[rows, :], wb_ref[ATTN_WIDTH:, :])
        x_new = x_ref[rows, :] + _rms(mixed, gpost_ref[...])
        xo_ref[rows, :] = x_new
        ho_ref[rows, :] = _rms(x_new, gnext_ref[...]).astype(ho_ref.dtype)


def _outproj(attn, ssm, w, layer, x, g_post, g_next, tm=512, sub=256):
    t, d = x.shape
    row = pl.BlockSpec((1, d), lambda m: (0, 0))
    return pl.pallas_call(
        functools.partial(_outproj_kernel, sub=sub),
        grid=(t // tm,),
        in_specs=[pl.BlockSpec((tm, ATTN_WIDTH), lambda m: (m, 0)),
                  pl.BlockSpec((tm, SSM_WIDTH), lambda m: (m, 0)),
                  pl.BlockSpec((None,) + w.shape[1:], lambda m: (layer, 0, 0), pipeline_mode=pl.Buffered(1)),
                  pl.BlockSpec((tm, d), lambda m: (m, 0)),
                  row, row],
        out_specs=[pl.BlockSpec((tm, d), lambda m: (m, 0)), pl.BlockSpec((tm, d), lambda m: (m, 0))],
        out_shape=[jax.ShapeDtypeStruct((t, d), F32), jax.ShapeDtypeStruct((t, d), BF16)],
        scratch_shapes=[pltpu.VMEM(w.shape[1:], BF16)],
        compiler_params=_params("arbitrary"),
        name="out_proj",
    )(attn, ssm, w, x, g_post.reshape(1, d), g_next.reshape(1, d))


def _ffn_up_kernel(h_ref, wg_ref, wu_ref, cwg_ref, cwu_ref, cbg_ref, cbu_ref, o_ref, wb_ref, buf_ref,
                   *, row_tiles):
    assert sum(row_tiles) == h_ref.shape[0]
    H = CONV_HALO
    tn = o_ref.shape[1]

    @pl.when(pl.program_id(1) == 0)
    def _():
        wb_ref[:, 0:tn] = wg_ref[...].astype(BF16)
        wb_ref[:, tn:] = wu_ref[...].astype(BF16)
        buf_ref[0:H, :] = jnp.zeros((H, buf_ref.shape[1]), F32)

    def conv(cols, cw_ref, cb_ref, r0, tm):
        u = buf_ref[r0:r0 + H + tm, cols]
        acc = cw_ref[0:1, :] * u
        for k in range(1, FFN_CONV):
            acc = pltpu.roll(acc, 1, 0) + cw_ref[k:k + 1, :] * u
        return acc[H:, :] + cb_ref[...]

    r0 = 0
    for tm in row_tiles:
        buf_ref[H + r0:H + r0 + tm, :] = _dot(h_ref[r0:r0 + tm, :], wb_ref[...])
        gate = conv(slice(0, tn), cwg_ref, cbg_ref, r0, tm)
        up = conv(slice(tn, 2 * tn), cwu_ref, cbu_ref, r0, tm)
        o_ref[r0:r0 + tm, :] = (jax.nn.gelu(gate, approximate=True) * up).astype(o_ref.dtype)
        r0 += tm


def _ffn_up(h, w_up, conv_w, conv_b, layer, seq, row_tiles=(1024, 1024), tn=512):
    t, d = h.shape
    nt = FFN_HIDDEN // tn
    col = lambda rows, half: pl.BlockSpec((None, rows, tn), lambda j, b: (layer, 0, half * nt + j))
    return pl.pallas_call(
        functools.partial(_ffn_up_kernel, row_tiles=row_tiles),
        grid=(nt, t // seq),
        in_specs=[pl.BlockSpec((seq, d), lambda j, b: (b, 0)),
                  col(d, 0), col(d, 1), col(FFN_CONV, 0), col(FFN_CONV, 1), col(1, 0), col(1, 1)],
        out_specs=pl.BlockSpec((seq, tn), lambda j, b: (b, j)),
        out_shape=jax.ShapeDtypeStruct((t, FFN_HIDDEN), BF16),
        scratch_shapes=[pltpu.VMEM((d, 2 * tn), BF16), pltpu.VMEM((seq + CONV_HALO, 2 * tn), F32)],
        compiler_params=_params("parallel", "arbitrary"),
        name="ffn_up_conv_geglu",
    )(h, w_up, w_up, conv_w, conv_w, conv_b, conv_b)


def _ffn_down_kernel(a_ref, w_ref, x_ref, gpost_ref, gnext_ref, xo_ref, ho_ref):
    x_new = x_ref[...] + _rms(_dot(a_ref[...], w_ref[...]), gpost_ref[...])
    xo_ref[...] = x_new
    ho_ref[...] = _rms(x_new, gnext_ref[...]).astype(ho_ref.dtype)


def _ffn_down(act, w, layer, x, g_post, g_next, tm=256):
    t, d = x.shape
    kdim = act.shape[1]
    row = pl.BlockSpec((1, d), lambda m: (0, 0))
    return pl.pallas_call(
        _ffn_down_kernel,
        grid=(t // tm,),
        in_specs=[pl.BlockSpec((tm, kdim), lambda m: (m, 0)),
                  pl.BlockSpec((None, kdim, d), lambda m: (layer, 0, 0), pipeline_mode=pl.Buffered(1)),
                  pl.BlockSpec((tm, d), lambda m: (m, 0)),
                  row, row],
        out_specs=[pl.BlockSpec((tm, d), lambda m: (m, 0)), pl.BlockSpec((tm, d), lambda m: (m, 0))],
        out_shape=[jax.ShapeDtypeStruct((t, d), F32), jax.ShapeDtypeStruct((t, d), BF16)],
        compiler_params=_params("parallel"),
        name="ffn_down",
    )(act, w, x, g_post.reshape(1, d), g_next.reshape(1, d))


def kernel(x, rel_bias, ln_mix_pre, w_in, ssm_conv_w, ssm_conv_b, dt_bias, a_log, d_skip, ssm_norm_w, w_out,
           ln_mix_post, ln_ffn_pre, w_ffn_up, ffn_conv_w, ffn_conv_b, w_ffn_down, ln_ffn_post):
    bsz, seq, d = x.shape
    depth = w_in.shape[0]
    t = bsz * seq
    assert w_in.shape[2] == QKV_COLS + REST_COLS + SSM_HEADS

    rel_bias_t = rel_bias.T.astype(F32)
    town, tprev = _bias_tiles(rel_bias_t)
    w_in_t = jnp.swapaxes(w_in, 1, 2)
    w_dt_t = jnp.pad(w_in_t[:, QKV_COLS + REST_COLS:, :], ((0, 0), (0, LANES - SSM_HEADS), (0, 0)))
    ffn_conv_b3 = ffn_conv_b.reshape(depth, 1, -1)
    w_down_bf16 = w_ffn_down.astype(BF16)

    xf = x.reshape(t, d)
    h = _rmsnorm(xf, ln_mix_pre[0])
    for l in range(depth):
        qkv, rest, dt_raw = _inproj(h, w_in_t, w_dt_t, l)
        attn = _moba(qkv.reshape(bsz, seq, QKV_COLS), rel_bias_t, town, tprev, bsz, seq)
        ssm = _ssd(rest.reshape(bsz, seq, REST_COLS), dt_raw.reshape(bsz, seq, LANES), ssm_conv_w[l],
                   ssm_conv_b[l], dt_bias[l], a_log[l], d_skip[l], ssm_norm_w[l], bsz, seq)
        xf, h = _outproj(attn.reshape(t, ATTN_WIDTH), ssm.reshape(t, SSM_WIDTH), w_out, l, xf,
                         ln_mix_post[l], ln_ffn_pre[l])
        act = _ffn_up(h, w_ffn_up, ffn_conv_w, ffn_conv_b3, l, seq)
        g_next = ln_mix_pre[(l + 1) % depth]
        xf, h = _ffn_down(act, w_down_bf16, l, xf, ln_ffn_post[l], g_next)
    return xf.reshape(bsz, seq, d)
```

```python
import functools
import math

import jax
import jax.numpy as jnp
import numpy as np
from jax import lax
from jax.experimental import pallas as pl
from jax.experimental.pallas import tpu as pltpu

F32 = jnp.float32
BF16 = jnp.bfloat16

LANES = 128
SUBLANES = 8
VMEM_LIMIT_BYTES = 56 * 1024 * 1024

D_MODEL = 2048
ATTN_HEAD_DIM = 128
ATTN_WIDTH = 1024
ATTN_HEADS = ATTN_WIDTH // ATTN_HEAD_DIM
MOBA_BLOCK = 256
MOBA_TOPK = 3
REL_BUCKETS = 32
REL_MAX_DIST = 128
SSM_WIDTH = 1024
SSM_HEAD_DIM = 64
SSM_HEADS = SSM_WIDTH // SSM_HEAD_DIM
SSM_GROUPS = 2
SSM_GROUP_WIDTH = SSM_WIDTH // SSM_GROUPS
SSM_STATE = 128
SSM_CONV = 4
SSM_CHUNK = 128
SSM_BC_WIDTH = 2 * SSM_GROUPS * SSM_STATE
FFN_HIDDEN = 5632
FFN_CONV = 3
NORM_EPS = 1e-6
NEG = -1e30
LOG2E = math.log2(math.e)

QKV_COLS = 3 * ATTN_WIDTH
REST_COLS = SSM_WIDTH + SSM_WIDTH + SSM_BC_WIDTH
CONV_HALO = SUBLANES


def _params(*semantics):
    return pltpu.CompilerParams(dimension_semantics=semantics, vmem_limit_bytes=VMEM_LIMIT_BYTES)


def _rms(x, w):
    return x * lax.rsqrt(jnp.mean(x * x, axis=-1, keepdims=True) + NORM_EPS) * w


def _split3(x):
    hi = x.astype(BF16)
    r1 = x - hi.astype(F32)
    mid = r1.astype(BF16)
    lo = (r1 - mid.astype(F32)).astype(BF16)
    return hi, mid, lo


def _dot(a, b):
    return jnp.dot(a, b, preferred_element_type=F32)


def _dot_nt(a, b):
    return lax.dot_general(a, b, (((1,), (1,)), ((), ())), preferred_element_type=F32)


def _rmsnorm_kernel(x_ref, w_ref, o_ref):
    o_ref[...] = _rms(x_ref[...], w_ref[...]).astype(o_ref.dtype)


def _rmsnorm(x, w, tm=512):
    t, d = x.shape
    return pl.pallas_call(
        _rmsnorm_kernel,
        grid=(t // tm,),
        in_specs=[pl.BlockSpec((tm, d), lambda m: (m, 0)),
                  pl.BlockSpec((1, d), lambda m: (0, 0))],
        out_specs=pl.BlockSpec((tm, d), lambda m: (m, 0)),
        out_shape=jax.ShapeDtypeStruct((t, d), BF16),
        compiler_params=_params("parallel"),
        name="rmsnorm",
    )(x, w.reshape(1, d))


IN_TN = 512
IN_QKV_TILES = QKV_COLS // IN_TN
IN_REST_TILES = REST_COLS // IN_TN


def _inproj_kernel(h_ref, wt_ref, wdt_ref, wdown_ref, qkv_ref, rest_ref, dt_ref, wdown_bf16_ref, wb_ref, wdtb_ref):
    m = pl.program_id(0)
    j = pl.program_id(1)

    @pl.when(m == 0)
    def _():
        wb_ref[j] = wt_ref[...].T.astype(BF16)

    @pl.when((m == 0) & (j == 0))
    def _():
        wdtb_ref[...] = wdt_ref[...].T.astype(BF16)

    def round_wdown_slice():
        wdown_bf16_ref[...] = wdown_ref[...].astype(BF16)

    @pl.when(j < IN_QKV_TILES)
    def _():
        qkv_ref[...] = _dot(h_ref[...], wb_ref[j]).astype(qkv_ref.dtype)
        round_wdown_slice()

    @pl.when(j >= IN_QKV_TILES)
    def _():
        rest_ref[...] = _dot(h_ref[...], wb_ref[j])
        round_wdown_slice()

    @pl.when(j == 0)
    def _():
        dt_ref[...] = _dot(h_ref[...], wdtb_ref[...])


def _inproj(h, w_in_t, w_dt_t, w_down, layer, tm=1024):
    t, d = h.shape
    n_tiles = IN_QKV_TILES + IN_REST_TILES
    n_steps = (t // tm) * n_tiles
    f_rows = w_down.shape[1] // n_steps
    assert f_rows * n_steps == w_down.shape[1] and f_rows % (2 * SUBLANES) == 0
    w_index = lambda m, j: (layer, jnp.where(m == 0, j, n_tiles - 1), 0)
    return pl.pallas_call(
        _inproj_kernel,
        grid=(t // tm, n_tiles),
        in_specs=[pl.BlockSpec((tm, d), lambda m, j: (m, 0)),
                  pl.BlockSpec((None, IN_TN, d), w_index),
                  pl.BlockSpec((None, LANES, d), lambda m, j: (layer, 0, 0)),
                  pl.BlockSpec((None, f_rows, d), lambda m, j: (layer, m * n_tiles + j, 0))],
        out_specs=[pl.BlockSpec((tm, IN_TN), lambda m, j: (m, jnp.minimum(j, IN_QKV_TILES - 1))),
                   pl.BlockSpec((tm, IN_TN), lambda m, j: (m, jnp.maximum(j - IN_QKV_TILES, 0))),
                   pl.BlockSpec((tm, LANES), lambda m, j: (m, 0)),
                   pl.BlockSpec((f_rows, d), lambda m, j: (m * n_tiles + j, 0))],
        out_shape=[jax.ShapeDtypeStruct((t, QKV_COLS), BF16),
                   jax.ShapeDtypeStruct((t, REST_COLS), F32),
                   jax.ShapeDtypeStruct((t, LANES), F32),
                   jax.ShapeDtypeStruct(w_down.shape[1:], BF16)],
        scratch_shapes=[pltpu.VMEM((n_tiles, d, IN_TN), BF16), pltpu.VMEM((d, LANES), BF16)],
        compiler_params=_params("arbitrary", "arbitrary"),
        name="in_proj",
    )(h, w_in_t, w_dt_t, w_down)


def _t5_bucket_np(rel):
    n = np.maximum(rel, 0)
    max_exact = REL_BUCKETS // 2
    nf = np.maximum(n, 1).astype(np.float32)
    large = max_exact + (np.log(nf / np.float32(max_exact)) / np.float32(math.log(REL_MAX_DIST / max_exact))
                         * np.float32(REL_BUCKETS - max_exact)).astype(np.int32)
    large = np.minimum(large, REL_BUCKETS - 1)
    return np.where(n < max_exact, n, large).astype(np.int32)


def _bias_tiles_kernel(rb_ref, bown_ref, bprev_ref, town_ref, tprev_ref):
    h = pl.program_id(0)
    bown = bown_ref[...]
    bprev = bprev_ref[...]
    town = jnp.full(bown.shape, NEG, F32)
    tprev = jnp.zeros(bprev.shape, F32)
    for b in range(REL_BUCKETS):
        val = rb_ref[h, b] * LOG2E
        town = jnp.where(bown == b, val, town)
        tprev = jnp.where(bprev == b, val, tprev)
    town_ref[...] = town
    tprev_ref[...] = tprev


def _bias_tiles(rel_bias_t):
    r = np.arange(MOBA_BLOCK)
    rel_own = r[None, :] - r[:, None]
    bown = np.where(rel_own >= 0, _t5_bucket_np(rel_own), -1).astype(np.int32)
    bprev = _t5_bucket_np(MOBA_BLOCK + rel_own)
    tile = pl.BlockSpec((MOBA_BLOCK, MOBA_BLOCK), lambda h: (0, 0))
    out_tile = pl.BlockSpec((None, MOBA_BLOCK, MOBA_BLOCK), lambda h: (h, 0, 0))
    shape = jax.ShapeDtypeStruct((ATTN_HEADS, MOBA_BLOCK, MOBA_BLOCK), F32)
    return pl.pallas_call(
        _bias_tiles_kernel,
        grid=(ATTN_HEADS,),
        in_specs=[pl.BlockSpec(memory_space=pltpu.SMEM), tile, tile],
        out_specs=[out_tile, out_tile],
        out_shape=[shape, shape],
        compiler_params=_params("parallel"),
        name="t5_bias_tiles",
    )(rel_bias_t, jnp.asarray(bown), jnp.asarray(bprev))


KM_ROWS = 2 * SUBLANES
MOBA_KEY_TILE = 256


def _moba_kernel(rb_ref, q_ref, k_ref, v_ref, town_ref, tprev_ref, o_ref, vt_ref, km_ref, s_ref):
    h = pl.program_id(1)
    nb = k_ref.shape[0] // MOBA_BLOCK
    scale = ATTN_HEAD_DIM ** -0.5 * LOG2E
    far_bias = rb_ref[h, REL_BUCKETS - 1] * LOG2E
    blocks = [slice(jb * MOBA_BLOCK, (jb + 1) * MOBA_BLOCK) for jb in range(nb)]
    tiles_per_block = MOBA_BLOCK // MOBA_KEY_TILE

    km_ref[...] = jnp.zeros(km_ref.shape, F32)
    for jb in range(nb):
        vt_ref[:, blocks[jb]] = v_ref[blocks[jb], :].astype(F32).T.astype(BF16)
        km_ref[jb:jb + 1, :] = jnp.mean(k_ref[blocks[jb], :].astype(F32), axis=0, keepdims=True)
    km = km_ref[...]
    km_hi = km.astype(BF16)
    km_lo = (km - km_hi.astype(F32)).astype(BF16)

    for i in range(nb):
        q = q_ref[blocks[i], :]
        gated = i > MOBA_TOPK
        if gated:
            gate_t = (_dot_nt(km_hi, q) + _dot_nt(km_lo, q))[0:SUBLANES, :]
            blk = lax.broadcasted_iota(jnp.int32, gate_t.shape, 0)
            rank = jnp.zeros(gate_t.shape, jnp.int32)
            for jp in range(i):
                row = gate_t[jp:jp + 1, :]
                beats = (row > gate_t) | ((row == gate_t) & (jp < blk))
                rank = rank + jnp.where(beats, 1, 0)
            mneg_t = jnp.where((rank < MOBA_TOPK) & (blk < i), 0.0, NEG)

        buf = i % 2
        key_tiles = [(j, kt) for j in range(i + 1) for kt in range(tiles_per_block)]
        m = None
        for n, (j, kt) in enumerate(key_tiles):
            in_blk = slice(kt * MOBA_KEY_TILE, (kt + 1) * MOBA_KEY_TILE)
            keys = slice(j * MOBA_BLOCK + in_blk.start, j * MOBA_BLOCK + in_blk.stop)
            s = _dot_nt(k_ref[keys, :], q) * scale
            if j == i:
                s = s + town_ref[in_blk, :]
            else:
                mrow = mneg_t[j:j + 1, :] if gated else None
                if j == i - 1:
                    s = s + tprev_ref[in_blk, :]
                    if gated:
                        s = s + mrow
                else:
                    s = s + (far_bias + mrow if gated else far_bias)
            s_ref[buf, n] = s
            cmax = jnp.max(s, axis=0, keepdims=True)
            m = cmax if m is None else jnp.maximum(m, cmax)

        l = jnp.zeros((1, MOBA_BLOCK), F32)
        acc = jnp.zeros((ATTN_HEAD_DIM, MOBA_BLOCK), F32)
        for n, (j, kt) in enumerate(key_tiles):
            keys = slice(j * MOBA_BLOCK + kt * MOBA_KEY_TILE, j * MOBA_BLOCK + (kt + 1) * MOBA_KEY_TILE)
            p = jnp.exp2(s_ref[buf, n] - m)
            l = l + jnp.sum(p, axis=0, keepdims=True)
            acc = acc + _dot(vt_ref[:, keys], p.astype(BF16))
        o_ref[blocks[i], :] = (acc / l).T.astype(o_ref.dtype)


def _moba(qkv, rel_bias_t, town, tprev, bsz, seq):
    nb = seq // MOBA_BLOCK
    assert (MOBA_BLOCK + MOBA_BLOCK) > REL_MAX_DIST and nb <= SUBLANES
    tile = pl.BlockSpec((None, MOBA_BLOCK, MOBA_BLOCK), lambda b, h: (h, 0, 0))
    head_cols = lambda part: pl.BlockSpec((None, seq, ATTN_HEAD_DIM), lambda b, h: (b, 0, part * ATTN_HEADS + h))
    return pl.pallas_call(
        _moba_kernel,
        grid=(bsz, ATTN_HEADS),
        in_specs=[pl.BlockSpec(memory_space=pltpu.SMEM), head_cols(0), head_cols(1), head_cols(2), tile, tile],
        out_specs=pl.BlockSpec((None, seq, ATTN_HEAD_DIM), lambda b, h: (b, 0, h)),
        out_shape=jax.ShapeDtypeStruct((bsz, seq, ATTN_WIDTH), BF16),
        scratch_shapes=[pltpu.VMEM((ATTN_HEAD_DIM, seq), BF16),
                        pltpu.VMEM((KM_ROWS, ATTN_HEAD_DIM), F32),
                        pltpu.VMEM((2, seq // MOBA_KEY_TILE, MOBA_KEY_TILE, MOBA_BLOCK), F32)],
        compiler_params=_params("parallel", "parallel"),
        name="moba_attention",
    )(rel_bias_t, qkv, qkv, qkv, town, tprev)


def _silu(x):
    return x * jax.nn.sigmoid(x)


def _ssd_kernel(z_ref, xs_ref, bc_ref, dt_ref, cwx_ref, cwbc_ref, cbx_ref, cbbc_ref, dtb_ref, alog_ref,
                dskip_ref, normw_ref, expand_ref, ltri_ref, o_ref, xbuf_ref, bcbuf_ref, st_ref, y_ref):
    L = SSM_CHUNK
    H = CONV_HALO
    rows_per_step = o_ref.shape[0]

    @pl.when(pl.program_id(1) == 0)
    def _():
        xbuf_ref[0:H, :] = jnp.zeros((H, xbuf_ref.shape[1]), F32)
        bcbuf_ref[0:H, :] = jnp.zeros((H, bcbuf_ref.shape[1]), F32)
        st_ref[...] = jnp.zeros(st_ref.shape, F32)

    xbuf_ref[H:H + rows_per_step, :] = xs_ref[...]
    bcbuf_ref[H:H + rows_per_step, :] = bc_ref[...]
    for r0 in range(0, rows_per_step, L):
        _ssd_chunk(r0, z_ref, dt_ref, cwx_ref, cwbc_ref, cbx_ref, cbbc_ref, dtb_ref, alog_ref, dskip_ref, normw_ref,
                   expand_ref, ltri_ref, o_ref, xbuf_ref, bcbuf_ref, st_ref, y_ref)
    xbuf_ref[0:H, :] = xbuf_ref[rows_per_step:rows_per_step + H, :]
    bcbuf_ref[0:H, :] = bcbuf_ref[rows_per_step:rows_per_step + H, :]


def _ssd_chunk(r0, z_ref, dt_ref, cwx_ref, cwbc_ref, cbx_ref, cbbc_ref, dtb_ref, alog_ref, dskip_ref, normw_ref,
               expand_ref, ltri_ref, o_ref, xbuf_ref, bcbuf_ref, st_ref, y_ref):
    L = SSM_CHUNK
    H = CONV_HALO
    rows = slice(r0, r0 + L)

    def conv_silu(buf_ref, w_ref, b_ref):
        u = buf_ref[r0:r0 + H + L, :]
        acc = w_ref[0:1, :] * u
        for k in range(1, SSM_CONV):
            acc = pltpu.roll(acc, 1, 0) + w_ref[k:k + 1, :] * u
        return _silu(acc[H:, :] + b_ref[...])

    xs = conv_silu(xbuf_ref, cwx_ref, cbx_ref)
    bc = conv_silu(bcbuf_ref, cwbc_ref, cbbc_ref)

    x_dt = dt_ref[rows, :] + dtb_ref[...]
    dt = jnp.maximum(x_dt, 0.0) + jnp.log1p(jnp.exp(-jnp.abs(x_dt)))
    da = dt * (-jnp.exp(alog_ref[...]) * LOG2E)
    ltri = ltri_ref[...]
    a_cs = sum(_dot(ltri, part) for part in _split3(da))
    a_last = a_cs[L - 1:L, :]
    decay_st = jnp.exp2(a_last - a_cs)
    exp_acs = jnp.exp2(a_cs)
    chunk_decay = jnp.broadcast_to(jnp.exp2(a_last), (2 * SUBLANES, LANES))

    stacked = jnp.concatenate([dt, decay_st, exp_acs, chunk_decay], axis=0)
    expand = expand_ref[...]
    wide = sum(_dot(part, expand) for part in _split3(stacked))
    dt_w = wide[0:L]
    decay_st_w = wide[L:2 * L]
    exp_acs_w = wide[2 * L:3 * L]
    chunk_decay_w = wide[3 * L:3 * L + 1]

    xc = xs * dt_w
    a_cs_t = a_cs.T
    row_i = lax.broadcasted_iota(jnp.int32, (L, L), 0)
    col_i = lax.broadcasted_iota(jnp.int32, (L, L), 1)
    causal = row_i >= col_i
    first_head = col_i < SSM_HEAD_DIM
    heads_per_group = SSM_HEADS // SSM_GROUPS

    for g in range(SSM_GROUPS):
        gsl = slice(g * SSM_GROUP_WIDTH, (g + 1) * SSM_GROUP_WIDTH)
        bg = bc[:, g * SSM_STATE:(g + 1) * SSM_STATE]
        cg = bc[:, (SSM_GROUPS + g) * SSM_STATE:(SSM_GROUPS + g + 1) * SSM_STATE].astype(BF16)
        cb = _dot_nt(cg, bg.astype(BF16))
        prev = st_ref[:, gsl]
        y_off = _dot(cg, prev.astype(BF16)) * exp_acs_w[:, gsl]
        st_ref[:, gsl] = prev * chunk_decay_w[:, gsl] + _dot(bg.T.astype(BF16),
                                                             (xc[:, gsl] * decay_st_w[:, gsl]).astype(BF16))
        for pr in range(heads_per_group // 2):
            h0 = g * heads_per_group + 2 * pr
            psl = slice(h0 * SSM_HEAD_DIM, (h0 + 2) * SSM_HEAD_DIM)
            xcp = xc[:, psl].astype(BF16)
            ys = []
            for hh in (h0, h0 + 1):
                seg = a_cs[:, hh:hh + 1] - a_cs_t[hh:hh + 1, :]
                decay = jnp.exp2(jnp.where(causal, seg, NEG))
                ys.append(_dot((cb * decay).astype(BF16), xcp))
            y_ref[rows, psl] = jnp.where(first_head, ys[0], ys[1]) + y_off[:, 2 * pr * SSM_HEAD_DIM:
                                                                            (2 * pr + 2) * SSM_HEAD_DIM]

    y = (y_ref[rows, :] + dskip_ref[...] * xs) * _silu(z_ref[rows, :])
    normw = normw_ref[...]
    for g in range(SSM_GROUPS):
        gsl = slice(g * SSM_GROUP_WIDTH, (g + 1) * SSM_GROUP_WIDTH)
        yg = y[:, gsl]
        yg = yg * lax.rsqrt(jnp.mean(yg * yg, axis=-1, keepdims=True) + NORM_EPS)
        o_ref[rows, gsl] = (yg * normw[:, gsl]).astype(o_ref.dtype)


def _ssd(rest, dt_raw, conv_w, conv_b, dt_bias, a_log, d_skip, norm_w, bsz, seq, chunks_per_step=4):
    L = SSM_CHUNK
    R = chunks_per_step * L
    pad_heads = LANES - SSM_HEADS
    expand = np.zeros((LANES, SSM_WIDTH), np.float32)
    for hh in range(SSM_HEADS):
        expand[hh, hh * SSM_HEAD_DIM:(hh + 1) * SSM_HEAD_DIM] = 1.0
    ltri = np.tril(np.ones((L, L), np.float32))

    row = lambda v: v.reshape(1, -1).astype(F32)
    const = lambda shape: pl.BlockSpec(shape, lambda b, c: (0, 0))
    args = (rest, rest, rest, dt_raw,
            conv_w[:, :SSM_WIDTH], conv_w[:, SSM_WIDTH:], row(conv_b[:SSM_WIDTH]), row(conv_b[SSM_WIDTH:]),
            row(jnp.pad(dt_bias, (0, pad_heads))), row(jnp.pad(a_log, (0, pad_heads))),
            row(jnp.repeat(d_skip, SSM_HEAD_DIM)), row(norm_w),
            jnp.asarray(expand, BF16), jnp.asarray(ltri, BF16))
    in_specs = [
        pl.BlockSpec((None, R, SSM_WIDTH), lambda b, c: (b, c, 0)),
        pl.BlockSpec((None, R, SSM_WIDTH), lambda b, c: (b, c, 1)),
        pl.BlockSpec((None, R, SSM_BC_WIDTH), lambda b, c: (b, c, 2 * SSM_WIDTH // SSM_BC_WIDTH)),
        pl.BlockSpec((None, R, LANES), lambda b, c: (b, c, 0)),
        const((SSM_CONV, SSM_WIDTH)), const((SSM_CONV, SSM_BC_WIDTH)),
        const((1, SSM_WIDTH)), const((1, SSM_BC_WIDTH)),
        const((1, LANES)), const((1, LANES)), const((1, SSM_WIDTH)), const((1, SSM_WIDTH)),
        const((LANES, SSM_WIDTH)), const((L, L)),
    ]
    return pl.pallas_call(
        _ssd_kernel,
        grid=(bsz, seq // R),
        in_specs=in_specs,
        out_specs=pl.BlockSpec((None, R, SSM_WIDTH), lambda b, c: (b, c, 0)),
        out_shape=jax.ShapeDtypeStruct((bsz, seq, SSM_WIDTH), BF16),
        scratch_shapes=[pltpu.VMEM((R + CONV_HALO, SSM_WIDTH), F32),
                        pltpu.VMEM((R + CONV_HALO, SSM_BC_WIDTH), F32),
                        pltpu.VMEM((SSM_STATE, SSM_WIDTH), F32),
                        pltpu.VMEM((R, SSM_WIDTH), F32)],
        compiler_params=_params("parallel", "arbitrary"),
        name="ssd_mixer",
    )(*args)


def _outproj_kernel(attn_ref, ssm_ref, w_ref, x_ref, gpost_ref, gnext_ref, xo_ref, ho_ref, wb_ref, *, sub):
    @pl.when(pl.program_id(0) == 0)
    def _():
        wb_ref[...] = w_ref[...].astype(BF16)

    for r0 in range(0, x_ref.shape[0], sub):
        rows = slice(r0, r0 + sub)
        mixed = _dot(attn_ref[rows, :], wb_ref[0:ATTN_WIDTH, :]) + _dot(ssm_ref[rows, :], wb_ref[ATTN_WIDTH:, :])
        x_new = x_ref[rows, :] + _rms(mixed, gpost_ref[...])
        xo_ref[rows, :] = x_new
        ho_ref[rows, :] = _rms(x_new, gnext_ref[...]).astype(ho_ref.dtype)


def _outproj(attn, ssm, w, layer, x, g_post, g_next, tm=512, sub=256):
    t, d = x.shape
    row = pl.BlockSpec((1, d), lambda m: (0, 0))
    return pl.pallas_call(
        functools.partial(_outproj_kernel, sub=sub),
        grid=(t // tm,),
        in_specs=[pl.BlockSpec((tm, ATTN_WIDTH), lambda m: (m, 0)),
                  pl.BlockSpec((tm, SSM_WIDTH), lambda m: (m, 0)),
                  pl.BlockSpec((None,) + w.shape[1:], lambda m: (layer, 0, 0), pipeline_mode=pl.Buffered(1)),
                  pl.BlockSpec((tm, d), lambda m: (m, 0)),
                  row, row],
        out_specs=[pl.BlockSpec((tm, d), lambda m: (m, 0)), pl.BlockSpec((tm, d), lambda m: (m, 0))],
        out_shape=[jax.ShapeDtypeStruct((t, d), F32), jax.ShapeDtypeStruct((t, d), BF16)],
        scratch_shapes=[pltpu.VMEM(w.shape[1:], BF16)],
        compiler_params=_params("arbitrary"),
        name="out_proj",
    )(attn, ssm, w, x, g_post.reshape(1, d), g_next.reshape(1, d))


def _ffn_up_kernel(h_ref, wg_ref, wu_ref, cwg_ref, cwu_ref, cbg_ref, cbu_ref, o_ref, wb_ref, buf_ref,
                   *, row_tiles):
    assert sum(row_tiles) == h_ref.shape[0]
    H = CONV_HALO
    tn = o_ref.shape[1]

    @pl.when(pl.program_id(1) == 0)
    def _():
        wb_ref[:, 0:tn] = wg_ref[...].astype(BF16)
        wb_ref[:, tn:] = wu_ref[...].astype(BF16)
        buf_ref[0:H, :] = jnp.zeros((H, buf_ref.shape[1]), F32)

    def conv(cols, cw_ref, cb_ref, r0, tm):
        u = buf_ref[r0:r0 + H + tm, cols]
        acc = cw_ref[0:1, :] * u
        for k in range(1, FFN_CONV):
            acc = pltpu.roll(acc, 1, 0) + cw_ref[k:k + 1, :] * u
        return acc[H:, :] + cb_ref[...]

    r0 = 0
    for tm in row_tiles:
        buf_ref[H + r0:H + r0 + tm, :] = _dot(h_ref[r0:r0 + tm, :], wb_ref[...])
        gate = conv(slice(0, tn), cwg_ref, cbg_ref, r0, tm)
        up = conv(slice(tn, 2 * tn), cwu_ref, cbu_ref, r0, tm)
        o_ref[r0:r0 + tm, :] = (jax.nn.gelu(gate, approximate=True) * up).astype(o_ref.dtype)
        r0 += tm


def _ffn_up(h, w_up, conv_w, conv_b, layer, seq, row_tiles=(1024, 1024), tn=512):
    t, d = h.shape
    nt = FFN_HIDDEN // tn
    col = lambda rows, half: pl.BlockSpec((None, rows, tn), lambda j, b: (layer, 0, half * nt + j))
    return pl.pallas_call(
        functools.partial(_ffn_up_kernel, row_tiles=row_tiles),
        grid=(nt, t // seq),
        in_specs=[pl.BlockSpec((seq, d), lambda j, b: (b, 0)),
                  col(d, 0), col(d, 1), col(FFN_CONV, 0), col(FFN_CONV, 1), col(1, 0), col(1, 1)],
        out_specs=pl.BlockSpec((seq, tn), lambda j, b: (b, j)),
        out_shape=jax.ShapeDtypeStruct((t, FFN_HIDDEN), BF16),
        scratch_shapes=[pltpu.VMEM((d, 2 * tn), BF16), pltpu.VMEM((seq + CONV_HALO, 2 * tn), F32)],
        compiler_params=_params("parallel", "arbitrary"),
        name="ffn_up_conv_geglu",
    )(h, w_up, w_up, conv_w, conv_w, conv_b, conv_b)


def _ffn_down_kernel(a_ref, w_ref, x_ref, gpost_ref, gnext_ref, xo_ref, ho_ref):
    x_new = x_ref[...] + _rms(_dot(a_ref[...], w_ref[...]), gpost_ref[...])
    xo_ref[...] = x_new
    ho_ref[...] = _rms(x_new, gnext_ref[...]).astype(ho_ref.dtype)


def _ffn_down(act, w, x, g_post, g_next, tm=256):
    t, d = x.shape
    kdim = act.shape[1]
    row = pl.BlockSpec((1, d), lambda m: (0, 0))
    return pl.pallas_call(
        _ffn_down_kernel,
        grid=(t // tm,),
        in_specs=[pl.BlockSpec((tm, kdim), lambda m: (m, 0)),
                  pl.BlockSpec((kdim, d), lambda m: (0, 0), pipeline_mode=pl.Buffered(1)),
                  pl.BlockSpec((tm, d), lambda m: (m, 0)),
                  row, row],
        out_specs=[pl.BlockSpec((tm, d), lambda m: (m, 0)), pl.BlockSpec((tm, d), lambda m: (m, 0))],
        out_shape=[jax.ShapeDtypeStruct((t, d), F32), jax.ShapeDtypeStruct((t, d), BF16)],
        compiler_params=_params("parallel"),
        name="ffn_down",
    )(act, w, x, g_post.reshape(1, d), g_next.reshape(1, d))


def kernel(x, rel_bias, ln_mix_pre, w_in, ssm_conv_w, ssm_conv_b, dt_bias, a_log, d_skip, ssm_norm_w, w_out,
           ln_mix_post, ln_ffn_pre, w_ffn_up, ffn_conv_w, ffn_conv_b, w_ffn_down, ln_ffn_post):
    bsz, seq, d = x.shape
    depth = w_in.shape[0]
    t = bsz * seq
    assert w_in.shape[2] == QKV_COLS + REST_COLS + SSM_HEADS

    rel_bias_t = rel_bias.T.astype(F32)
    town, tprev = _bias_tiles(rel_bias_t)
    w_in_t = jnp.swapaxes(w_in, 1, 2)
    w_dt_t = jnp.pad(w_in_t[:, QKV_COLS + REST_COLS:, :], ((0, 0), (0, LANES - SSM_HEADS), (0, 0)))
    ffn_conv_b3 = ffn_conv_b.reshape(depth, 1, -1)

    xf = x.reshape(t, d)
    h = _rmsnorm(xf, ln_mix_pre[0])
    for l in range(depth):
        qkv, rest, dt_raw, w_down_bf16 = _inproj(h, w_in_t, w_dt_t, w_ffn_down, l)
        attn = _moba(qkv.reshape(bsz, seq, QKV_COLS), rel_bias_t, town, tprev, bsz, seq)
        ssm = _ssd(rest.reshape(bsz, seq, REST_COLS), dt_raw.reshape(bsz, seq, LANES), ssm_conv_w[l],
                   ssm_conv_b[l], dt_bias[l], a_log[l], d_skip[l], ssm_norm_w[l], bsz, seq)
        xf, h = _outproj(attn.reshape(t, ATTN_WIDTH), ssm.reshape(t, SSM_WIDTH), w_out, l, xf,
                         ln_mix_post[l], ln_ffn_pre[l])
        act = _ffn_up(h, w_ffn_up, ffn_conv_w, ffn_conv_b3, l, seq)
        g_next = ln_mix_pre[(l + 1) % depth]
        xf, h = _ffn_down(act, w_down_bf16, xf, ln_ffn_post[l], g_next)
    return xf.reshape(bsz, seq, d)
```

```python
import functools
import math

import jax
import jax.numpy as jnp
import numpy as np
from jax import lax
from jax.experimental import pallas as pl
from jax.experimental.pallas import tpu as pltpu

F32 = jnp.float32
BF16 = jnp.bfloat16

LANES = 128
SUBLANES = 8
VMEM_LIMIT_BYTES = 56 * 1024 * 1024

D_MODEL = 2048
ATTN_HEAD_DIM = 128
ATTN_WIDTH = 1024
ATTN_HEADS = ATTN_WIDTH // ATTN_HEAD_DIM
MOBA_BLOCK = 256
MOBA_TOPK = 3
REL_BUCKETS = 32
REL_MAX_DIST = 128
SSM_WIDTH = 1024
SSM_HEAD_DIM = 64
SSM_HEADS = SSM_WIDTH // SSM_HEAD_DIM
SSM_GROUPS = 2
SSM_GROUP_WIDTH = SSM_WIDTH // SSM_GROUPS
SSM_STATE = 128
SSM_CONV = 4
SSM_CHUNK = 128
SSM_BC_WIDTH = 2 * SSM_GROUPS * SSM_STATE
FFN_HIDDEN = 5632
FFN_CONV = 3
NORM_EPS = 1e-6
NEG = -1e30
LOG2E = math.log2(math.e)

QKV_COLS = 3 * ATTN_WIDTH
REST_COLS = SSM_WIDTH + SSM_WIDTH + SSM_BC_WIDTH
CONV_HALO = SUBLANES


def _params(*semantics):
    return pltpu.CompilerParams(dimension_semantics=semantics, vmem_limit_bytes=VMEM_LIMIT_BYTES)


def _rms(x, w):
    return x * lax.rsqrt(jnp.mean(x * x, axis=-1, keepdims=True) + NORM_EPS) * w


def _split3(x):
    hi = x.astype(BF16)
    r1 = x - hi.astype(F32)
    mid = r1.astype(BF16)
    lo = (r1 - mid.astype(F32)).astype(BF16)
    return hi, mid, lo


def _dot(a, b):
    return jnp.dot(a, b, preferred_element_type=F32)


def _dot_nt(a, b):
    return lax.dot_general(a, b, (((1,), (1,)), ((), ())), preferred_element_type=F32)


def _rmsnorm_kernel(x_ref, w_ref, o_ref):
    o_ref[...] = _rms(x_ref[...], w_ref[...]).astype(o_ref.dtype)


def _rmsnorm(x, w, tm=512):
    t, d = x.shape
    return pl.pallas_call(
        _rmsnorm_kernel,
        grid=(t // tm,),
        in_specs=[pl.BlockSpec((tm, d), lambda m: (m, 0)),
                  pl.BlockSpec((1, d), lambda m: (0, 0))],
        out_specs=pl.BlockSpec((tm, d), lambda m: (m, 0)),
        out_shape=jax.ShapeDtypeStruct((t, d), BF16),
        compiler_params=_params("parallel"),
        name="rmsnorm",
    )(x, w.reshape(1, d))


IN_TN = 512
IN_QKV_TILES = QKV_COLS // IN_TN
IN_REST_TILES = REST_COLS // IN_TN


def _inproj_kernel(h_ref, wt_ref, wdt_ref, wdown_ref, qkv_ref, rest_ref, dt_ref, wdown_bf16_ref, wb_ref, wdtb_ref):
    m = pl.program_id(0)
    j = pl.program_id(1)

    @pl.when(m == 0)
    def _():
        wb_ref[j] = wt_ref[...].T.astype(BF16)

    @pl.when((m == 0) & (j == 0))
    def _():
        wdtb_ref[...] = wdt_ref[...].T.astype(BF16)

    def round_wdown_slice():
        wdown_bf16_ref[...] = wdown_ref[...].astype(BF16)

    @pl.when(j < IN_QKV_TILES)
    def _():
        qkv_ref[...] = _dot(h_ref[...], wb_ref[j]).astype(qkv_ref.dtype)
        round_wdown_slice()

    @pl.when(j >= IN_QKV_TILES)
    def _():
        rest_ref[...] = _dot(h_ref[...], wb_ref[j])
        round_wdown_slice()

    @pl.when(j == 0)
    def _():
        dt_ref[...] = _dot(h_ref[...], wdtb_ref[...])


def _inproj(h, w_in_t, w_dt_t, w_down, layer, tm=1024):
    t, d = h.shape
    n_tiles = IN_QKV_TILES + IN_REST_TILES
    n_steps = (t // tm) * n_tiles
    f_rows = w_down.shape[1] // n_steps
    assert f_rows * n_steps == w_down.shape[1] and f_rows % (2 * SUBLANES) == 0
    w_index = lambda m, j: (layer, jnp.where(m == 0, j, n_tiles - 1), 0)
    return pl.pallas_call(
        _inproj_kernel,
        grid=(t // tm, n_tiles),
        in_specs=[pl.BlockSpec((tm, d), lambda m, j: (m, 0)),
                  pl.BlockSpec((None, IN_TN, d), w_index),
                  pl.BlockSpec((None, LANES, d), lambda m, j: (layer, 0, 0)),
                  pl.BlockSpec((None, f_rows, d), lambda m, j: (layer, m * n_tiles + j, 0))],
        out_specs=[pl.BlockSpec((tm, IN_TN), lambda m, j: (m, jnp.minimum(j, IN_QKV_TILES - 1))),
                   pl.BlockSpec((tm, IN_TN), lambda m, j: (m, jnp.maximum(j - IN_QKV_TILES, 0))),
                   pl.BlockSpec((tm, LANES), lambda m, j: (m, 0)),
                   pl.BlockSpec((f_rows, d), lambda m, j: (m * n_tiles + j, 0))],
        out_shape=[jax.ShapeDtypeStruct((t, QKV_COLS), BF16),
                   jax.ShapeDtypeStruct((t, REST_COLS), F32),
                   jax.ShapeDtypeStruct((t, LANES), F32),
                   jax.ShapeDtypeStruct(w_down.shape[1:], BF16)],
        scratch_shapes=[pltpu.VMEM((n_tiles, d, IN_TN), BF16), pltpu.VMEM((d, LANES), BF16)],
        compiler_params=_params("arbitrary", "arbitrary"),
        name="in_proj",
    )(h, w_in_t, w_dt_t, w_down)


def _t5_bucket_np(rel):
    n = np.maximum(rel, 0)
    max_exact = REL_BUCKETS // 2
    nf = np.maximum(n, 1).astype(np.float32)
    large = max_exact + (np.log(nf / np.float32(max_exact)) / np.float32(math.log(REL_MAX_DIST / max_exact))
                         * np.float32(REL_BUCKETS - max_exact)).astype(np.int32)
    large = np.minimum(large, REL_BUCKETS - 1)
    return np.where(n < max_exact, n, large).astype(np.int32)


def _bias_tiles_kernel(rb_ref, bown_ref, bprev_ref, town_ref, tprev_ref):
    h = pl.program_id(0)
    bown = bown_ref[...]
    bprev = bprev_ref[...]
    town = jnp.full(bown.shape, NEG, F32)
    tprev = jnp.zeros(bprev.shape, F32)
    for b in range(REL_BUCKETS):
        val = rb_ref[h, b] * LOG2E
        town = jnp.where(bown == b, val, town)
        tprev = jnp.where(bprev == b, val, tprev)
    town_ref[...] = town
    tprev_ref[...] = tprev


def _bias_tiles(rel_bias_t):
    r = np.arange(MOBA_BLOCK)
    rel_own = r[None, :] - r[:, None]
    bown = np.where(rel_own >= 0, _t5_bucket_np(rel_own), -1).astype(np.int32)
    bprev = _t5_bucket_np(MOBA_BLOCK + rel_own)
    tile = pl.BlockSpec((MOBA_BLOCK, MOBA_BLOCK), lambda h: (0, 0))
    out_tile = pl.BlockSpec((None, MOBA_BLOCK, MOBA_BLOCK), lambda h: (h, 0, 0))
    shape = jax.ShapeDtypeStruct((ATTN_HEADS, MOBA_BLOCK, MOBA_BLOCK), F32)
    return pl.pallas_call(
        _bias_tiles_kernel,
        grid=(ATTN_HEADS,),
        in_specs=[pl.BlockSpec(memory_space=pltpu.SMEM), tile, tile],
        out_specs=[out_tile, out_tile],
        out_shape=[shape, shape],
        compiler_params=_params("parallel"),
        name="t5_bias_tiles",
    )(rel_bias_t, jnp.asarray(bown), jnp.asarray(bprev))


KM_ROWS = 2 * SUBLANES
MOBA_KEY_TILE = 256
MOBA_HEADS_PER_STEP = 2


def _moba_kernel(rb_ref, q_ref, k_ref, v_ref, town_ref, tprev_ref, o_ref, vt_ref, km_ref, s_ref):
    for hh in range(MOBA_HEADS_PER_STEP):
        _moba_head(hh, pl.program_id(1) * MOBA_HEADS_PER_STEP + hh, rb_ref, q_ref, k_ref, v_ref, town_ref, tprev_ref,
                   o_ref, vt_ref, km_ref, s_ref)


def _moba_head(hh, h, rb_ref, q_ref, k_ref, v_ref, town_ref, tprev_ref, o_ref, vt_ref, km_ref, s_ref):
    cols = slice(hh * ATTN_HEAD_DIM, (hh + 1) * ATTN_HEAD_DIM)
    nb = k_ref.shape[0] // MOBA_BLOCK
    scale = ATTN_HEAD_DIM ** -0.5 * LOG2E
    far_bias = rb_ref[h, REL_BUCKETS - 1] * LOG2E
    blocks = [slice(jb * MOBA_BLOCK, (jb + 1) * MOBA_BLOCK) for jb in range(nb)]
    tiles_per_block = MOBA_BLOCK // MOBA_KEY_TILE

    km_ref[hh] = jnp.zeros(km_ref.shape[1:], F32)
    for jb in range(nb):
        vt_ref[hh, :, blocks[jb]] = v_ref[blocks[jb], cols].astype(F32).T.astype(BF16)
        km_ref[hh, jb:jb + 1, :] = jnp.mean(k_ref[blocks[jb], cols].astype(F32), axis=0, keepdims=True)
    km = km_ref[hh]
    km_hi = km.astype(BF16)
    km_lo = (km - km_hi.astype(F32)).astype(BF16)

    for i in range(nb):
        q = q_ref[blocks[i], cols]
        gated = i > MOBA_TOPK
        if gated:
            gate_t = (_dot_nt(km_hi, q) + _dot_nt(km_lo, q))[0:SUBLANES, :]
            blk = lax.broadcasted_iota(jnp.int32, gate_t.shape, 0)
            rank = jnp.zeros(gate_t.shape, jnp.int32)
            for jp in range(i):
                row = gate_t[jp:jp + 1, :]
                beats = (row > gate_t) | ((row == gate_t) & (jp < blk))
                rank = rank + jnp.where(beats, 1, 0)
            mneg_t = jnp.where((rank < MOBA_TOPK) & (blk < i), 0.0, NEG)

        buf = i % 2
        key_tiles = [(j, kt) for j in range(i + 1) for kt in range(tiles_per_block)]
        m = None
        for n, (j, kt) in enumerate(key_tiles):
            in_blk = slice(kt * MOBA_KEY_TILE, (kt + 1) * MOBA_KEY_TILE)
            keys = slice(j * MOBA_BLOCK + in_blk.start, j * MOBA_BLOCK + in_blk.stop)
            s = _dot_nt(k_ref[keys, cols], q) * scale
            if j == i:
                s = s + town_ref[hh, in_blk, :]
            else:
                mrow = mneg_t[j:j + 1, :] if gated else None
                if j == i - 1:
                    s = s + tprev_ref[hh, in_blk, :]
                    if gated:
                        s = s + mrow
                else:
                    s = s + (far_bias + mrow if gated else far_bias)
            s_ref[hh, buf, n] = s
            cmax = jnp.max(s, axis=0, keepdims=True)
            m = cmax if m is None else jnp.maximum(m, cmax)

        l = jnp.zeros((1, MOBA_BLOCK), F32)
        acc = jnp.zeros((ATTN_HEAD_DIM, MOBA_BLOCK), F32)
        for n, (j, kt) in enumerate(key_tiles):
            keys = slice(j * MOBA_BLOCK + kt * MOBA_KEY_TILE, j * MOBA_BLOCK + (kt + 1) * MOBA_KEY_TILE)
            p = jnp.exp2(s_ref[hh, buf, n] - m)
            l = l + jnp.sum(p, axis=0, keepdims=True)
            acc = acc + _dot(vt_ref[hh, :, keys], p.astype(BF16))
        o_ref[blocks[i], cols] = (acc / l).T.astype(o_ref.dtype)


def _moba(qkv, rel_bias_t, town, tprev, bsz, seq):
    nb = seq // MOBA_BLOCK
    hps = MOBA_HEADS_PER_STEP
    width = hps * ATTN_HEAD_DIM
    groups = ATTN_HEADS // hps
    assert (MOBA_BLOCK + MOBA_BLOCK) > REL_MAX_DIST and nb <= SUBLANES
    tile = pl.BlockSpec((hps, MOBA_BLOCK, MOBA_BLOCK), lambda b, g: (g, 0, 0))
    head_cols = lambda part: pl.BlockSpec((None, seq, width), lambda b, g: (b, 0, part * groups + g))
    return pl.pallas_call(
        _moba_kernel,
        grid=(bsz, groups),
        in_specs=[pl.BlockSpec(memory_space=pltpu.SMEM), head_cols(0), head_cols(1), head_cols(2), tile, tile],
        out_specs=pl.BlockSpec((None, seq, width), lambda b, g: (b, 0, g)),
        out_shape=jax.ShapeDtypeStruct((bsz, seq, ATTN_WIDTH), BF16),
        scratch_shapes=[pltpu.VMEM((hps, ATTN_HEAD_DIM, seq), BF16),
                        pltpu.VMEM((hps, KM_ROWS, ATTN_HEAD_DIM), F32),
                        pltpu.VMEM((hps, 2, seq // MOBA_KEY_TILE, MOBA_KEY_TILE, MOBA_BLOCK), F32)],
        compiler_params=_params("parallel", "parallel"),
        name="moba_attention",
    )(rel_bias_t, qkv, qkv, qkv, town, tprev)


def _silu(x):
    return x * jax.nn.sigmoid(x)


def _ssd_kernel(z_ref, xs_ref, bc_ref, dt_ref, cwx_ref, cwbc_ref, cbx_ref, cbbc_ref, dtb_ref, alog_ref,
                dskip_ref, normw_ref, expand_ref, ltri_ref, o_ref, xbuf_ref, bcbuf_ref, st_ref, y_ref):
    L = SSM_CHUNK
    H = CONV_HALO
    rows_per_step = o_ref.shape[0]

    @pl.when(pl.program_id(1) == 0)
    def _():
        xbuf_ref[0:H, :] = jnp.zeros((H, xbuf_ref.shape[1]), F32)
        bcbuf_ref[0:H, :] = jnp.zeros((H, bcbuf_ref.shape[1]), F32)
        st_ref[...] = jnp.zeros(st_ref.shape, F32)

    xbuf_ref[H:H + rows_per_step, :] = xs_ref[...]
    bcbuf_ref[H:H + rows_per_step, :] = bc_ref[...]
    for r0 in range(0, rows_per_step, L):
        _ssd_chunk(r0, z_ref, dt_ref, cwx_ref, cwbc_ref, cbx_ref, cbbc_ref, dtb_ref, alog_ref, dskip_ref, normw_ref,
                   expand_ref, ltri_ref, o_ref, xbuf_ref, bcbuf_ref, st_ref, y_ref)
    xbuf_ref[0:H, :] = xbuf_ref[rows_per_step:rows_per_step + H, :]
    bcbuf_ref[0:H, :] = bcbuf_ref[rows_per_step:rows_per_step + H, :]


def _ssd_chunk(r0, z_ref, dt_ref, cwx_ref, cwbc_ref, cbx_ref, cbbc_ref, dtb_ref, alog_ref, dskip_ref, normw_ref,
               expand_ref, ltri_ref, o_ref, xbuf_ref, bcbuf_ref, st_ref, y_ref):
    L = SSM_CHUNK
    H = CONV_HALO
    rows = slice(r0, r0 + L)

    def conv_silu(buf_ref, w_ref, b_ref):
        u = buf_ref[r0:r0 + H + L, :]
        acc = w_ref[0:1, :] * u
        for k in range(1, SSM_CONV):
            acc = pltpu.roll(acc, 1, 0) + w_ref[k:k + 1, :] * u
        return _silu(acc[H:, :] + b_ref[...])

    xs = conv_silu(xbuf_ref, cwx_ref, cbx_ref)
    bc = conv_silu(bcbuf_ref, cwbc_ref, cbbc_ref)

    x_dt = dt_ref[rows, :] + dtb_ref[...]
    dt = jnp.maximum(x_dt, 0.0) + jnp.log1p(jnp.exp(-jnp.abs(x_dt)))
    da = dt * (-jnp.exp(alog_ref[...]) * LOG2E)
    ltri = ltri_ref[...]
    a_cs = sum(_dot(ltri, part) for part in _split3(da))
    a_last = a_cs[L - 1:L, :]
    decay_st = jnp.exp2(a_last - a_cs)
    exp_acs = jnp.exp2(a_cs)
    chunk_decay = jnp.broadcast_to(jnp.exp2(a_last), (2 * SUBLANES, LANES))

    stacked = jnp.concatenate([dt, decay_st, exp_acs, chunk_decay], axis=0)
    expand = expand_ref[...]
    wide = sum(_dot(part, expand) for part in _split3(stacked))
    dt_w = wide[0:L]
    decay_st_w = wide[L:2 * L]
    exp_acs_w = wide[2 * L:3 * L]
    chunk_decay_w = wide[3 * L:3 * L + 1]

    xc = xs * dt_w
    a_cs_t = a_cs.T
    row_i = lax.broadcasted_iota(jnp.int32, (L, L), 0)
    col_i = lax.broadcasted_iota(jnp.int32, (L, L), 1)
    causal = row_i >= col_i
    first_head = col_i < SSM_HEAD_DIM
    heads_per_group = SSM_HEADS // SSM_GROUPS

    for g in range(SSM_GROUPS):
        gsl = slice(g * SSM_GROUP_WIDTH, (g + 1) * SSM_GROUP_WIDTH)
        bg = bc[:, g * SSM_STATE:(g + 1) * SSM_STATE]
        cg = bc[:, (SSM_GROUPS + g) * SSM_STATE:(SSM_GROUPS + g + 1) * SSM_STATE].astype(BF16)
        cb = _dot_nt(cg, bg.astype(BF16))
        prev = st_ref[:, gsl]
        y_off = _dot(cg, prev.astype(BF16)) * exp_acs_w[:, gsl]
        st_ref[:, gsl] = prev * chunk_decay_w[:, gsl] + _dot(bg.T.astype(BF16),
                                                             (xc[:, gsl] * decay_st_w[:, gsl]).astype(BF16))
        for pr in range(heads_per_group // 2):
            h0 = g * heads_per_group + 2 * pr
            psl = slice(h0 * SSM_HEAD_DIM, (h0 + 2) * SSM_HEAD_DIM)
            xcp = xc[:, psl].astype(BF16)
            ys = []
            for hh in (h0, h0 + 1):
                seg = a_cs[:, hh:hh + 1] - a_cs_t[hh:hh + 1, :]
                decay = jnp.exp2(jnp.where(causal, seg, NEG))
                ys.append(_dot((cb * decay).astype(BF16), xcp))
            y_ref[rows, psl] = jnp.where(first_head, ys[0], ys[1]) + y_off[:, 2 * pr * SSM_HEAD_DIM:
                                                                            (2 * pr + 2) * SSM_HEAD_DIM]

    y = (y_ref[rows, :] + dskip_ref[...] * xs) * _silu(z_ref[rows, :])
    normw = normw_ref[...]
    for g in range(SSM_GROUPS):
        gsl = slice(g * SSM_GROUP_WIDTH, (g + 1) * SSM_GROUP_WIDTH)
        yg = y[:, gsl]
        yg = yg * lax.rsqrt(jnp.mean(yg * yg, axis=-1, keepdims=True) + NORM_EPS)
        o_ref[rows, gsl] = (yg * normw[:, gsl]).astype(o_ref.dtype)


def _ssd(rest, dt_raw, conv_w, conv_b, dt_bias, a_log, d_skip, norm_w, bsz, seq, chunks_per_step=4):
    L = SSM_CHUNK
    R = chunks_per_step * L
    pad_heads = LANES - SSM_HEADS
    expand = np.zeros((LANES, SSM_WIDTH), np.float32)
    for hh in range(SSM_HEADS):
        expand[hh, hh * SSM_HEAD_DIM:(hh + 1) * SSM_HEAD_DIM] = 1.0
    ltri = np.tril(np.ones((L, L), np.float32))

    row = lambda v: v.reshape(1, -1).astype(F32)
    const = lambda shape: pl.BlockSpec(shape, lambda b, c: (0, 0))
    args = (rest, rest, rest, dt_raw,
            conv_w[:, :SSM_WIDTH], conv_w[:, SSM_WIDTH:], row(conv_b[:SSM_WIDTH]), row(conv_b[SSM_WIDTH:]),
            row(jnp.pad(dt_bias, (0, pad_heads))), row(jnp.pad(a_log, (0, pad_heads))),
            row(jnp.repeat(d_skip, SSM_HEAD_DIM)), row(norm_w),
            jnp.asarray(expand, BF16), jnp.asarray(ltri, BF16))
    in_specs = [
        pl.BlockSpec((None, R, SSM_WIDTH), lambda b, c: (b, c, 0)),
        pl.BlockSpec((None, R, SSM_WIDTH), lambda b, c: (b, c, 1)),
        pl.BlockSpec((None, R, SSM_BC_WIDTH), lambda b, c: (b, c, 2 * SSM_WIDTH // SSM_BC_WIDTH)),
        pl.BlockSpec((None, R, LANES), lambda b, c: (b, c, 0)),
        const((SSM_CONV, SSM_WIDTH)), const((SSM_CONV, SSM_BC_WIDTH)),
        const((1, SSM_WIDTH)), const((1, SSM_BC_WIDTH)),
        const((1, LANES)), const((1, LANES)), const((1, SSM_WIDTH)), const((1, SSM_WIDTH)),
        const((LANES, SSM_WIDTH)), const((L, L)),
    ]
    return pl.pallas_call(
        _ssd_kernel,
        grid=(bsz, seq // R),
        in_specs=in_specs,
        out_specs=pl.BlockSpec((None, R, SSM_WIDTH), lambda b, c: (b, c, 0)),
        out_shape=jax.ShapeDtypeStruct((bsz, seq, SSM_WIDTH), BF16),
        scratch_shapes=[pltpu.VMEM((R + CONV_HALO, SSM_WIDTH), F32),
                        pltpu.VMEM((R + CONV_HALO, SSM_BC_WIDTH), F32),
                        pltpu.VMEM((SSM_STATE, SSM_WIDTH), F32),
                        pltpu.VMEM((R, SSM_WIDTH), F32)],
        compiler_params=_params("parallel", "arbitrary"),
        name="ssd_mixer",
    )(*args)


def _outproj_kernel(attn_ref, ssm_ref, w_ref, x_ref, gpost_ref, gnext_ref, xo_ref, ho_ref, wb_ref, *, sub):
    @pl.when(pl.program_id(0) == 0)
    def _():
        wb_ref[...] = w_ref[...].astype(BF16)

    for r0 in range(0, x_ref.shape[0], sub):
        rows = slice(r0, r0 + sub)
        mixed = _dot(attn_ref[rows, :], wb_ref[0:ATTN_WIDTH, :]) + _dot(ssm_ref[rows, :], wb_ref[ATTN_WIDTH:, :])
        x_new = x_ref[rows, :] + _rms(mixed, gpost_ref[...])
        xo_ref[rows, :] = x_new
        ho_ref[rows, :] = _rms(x_new, gnext_ref[...]).astype(ho_ref.dtype)


def _outproj(attn, ssm, w, layer, x, g_post, g_next, tm=512, sub=256):
    t, d = x.shape
    row = pl.BlockSpec((1, d), lambda m: (0, 0))
    return pl.pallas_call(
        functools.partial(_outproj_kernel, sub=sub),
        grid=(t // tm,),
        in_specs=[pl.BlockSpec((tm, ATTN_WIDTH), lambda m: (m, 0)),
                  pl.BlockSpec((tm, SSM_WIDTH), lambda m: (m, 0)),
                  pl.BlockSpec((None,) + w.shape[1:], lambda m: (layer, 0, 0), pipeline_mode=pl.Buffered(1)),
                  pl.BlockSpec((tm, d), lambda m: (m, 0)),
                  row, row],
        out_specs=[pl.BlockSpec((tm, d), lambda m: (m, 0)), pl.BlockSpec((tm, d), lambda m: (m, 0))],
        out_shape=[jax.ShapeDtypeStruct((t, d), F32), jax.ShapeDtypeStruct((t, d), BF16)],
        scratch_shapes=[pltpu.VMEM(w.shape[1:], BF16)],
        compiler_params=_params("arbitrary"),
        name="out_proj",
    )(attn, ssm, w, x, g_post.reshape(1, d), g_next.reshape(1, d))


def _ffn_up_kernel(h_ref, wg_ref, wu_ref, cwg_ref, cwu_ref, cbg_ref, cbu_ref, o_ref, wb_ref, *, row_tiles):
    assert sum(row_tiles) == h_ref.shape[0]
    H = CONV_HALO
    tn = o_ref.shape[1]

    @pl.when(pl.program_id(1) == 0)
    def _():
        wb_ref[:, 0:tn] = wg_ref[...].astype(BF16)
        wb_ref[:, tn:] = wu_ref[...].astype(BF16)

    def conv(u, cw_ref, cb_ref):
        acc = cw_ref[0:1, :] * u
        for k in range(1, FFN_CONV):
            acc = pltpu.roll(acc, 1, 0) + cw_ref[k:k + 1, :] * u
        return acc[H:, :] + cb_ref[...]

    r0 = 0
    tail = jnp.zeros((H, 2 * tn), F32)
    for tm in row_tiles:
        res = _dot(h_ref[r0:r0 + tm, :], wb_ref[...])
        u = jnp.concatenate([tail, res], axis=0)
        gate = conv(u[:, 0:tn], cwg_ref, cbg_ref)
        up = conv(u[:, tn:], cwu_ref, cbu_ref)
        o_ref[r0:r0 + tm, :] = (jax.nn.gelu(gate, approximate=True) * up).astype(o_ref.dtype)
        tail = res[tm - H:, :]
        r0 += tm


def _ffn_up(h, w_up, conv_w, conv_b, layer, seq, row_tiles=(1024, 1024), tn=512):
    t, d = h.shape
    nt = FFN_HIDDEN // tn
    col = lambda rows, half: pl.BlockSpec((None, rows, tn), lambda j, b: (layer, 0, half * nt + j))
    return pl.pallas_call(
        functools.partial(_ffn_up_kernel, row_tiles=row_tiles),
        grid=(nt, t // seq),
        in_specs=[pl.BlockSpec((seq, d), lambda j, b: (b, 0)),
                  col(d, 0), col(d, 1), col(FFN_CONV, 0), col(FFN_CONV, 1), col(1, 0), col(1, 1)],
        out_specs=pl.BlockSpec((seq, tn), lambda j, b: (b, j)),
        out_shape=jax.ShapeDtypeStruct((t, FFN_HIDDEN), BF16),
        scratch_shapes=[pltpu.VMEM((d, 2 * tn), BF16)],
        compiler_params=_params("parallel", "arbitrary"),
        name="ffn_up_conv_geglu",
    )(h, w_up, w_up, conv_w, conv_w, conv_b, conv_b)


def _ffn_down_kernel(a_ref, w_ref, x_ref, gpost_ref, gnext_ref, xo_ref, ho_ref, *, sub):
    for r0 in range(0, x_ref.shape[0], sub):
        rows = slice(r0, r0 + sub)
        x_new = x_ref[rows, :] + _rms(_dot(a_ref[rows, :], w_ref[...]), gpost_ref[...])
        xo_ref[rows, :] = x_new
        ho_ref[rows, :] = _rms(x_new, gnext_ref[...]).astype(ho_ref.dtype)


def _ffn_down(act, w, x, g_post, g_next, tm=512, sub=256):
    t, d = x.shape
    kdim = act.shape[1]
    row = pl.BlockSpec((1, d), lambda m: (0, 0))
    return pl.pallas_call(
        functools.partial(_ffn_down_kernel, sub=sub),
        grid=(t // tm,),
        in_specs=[pl.BlockSpec((tm, kdim), lambda m: (m, 0)),
                  pl.BlockSpec((kdim, d), lambda m: (0, 0), pipeline_mode=pl.Buffered(1)),
                  pl.BlockSpec((tm, d), lambda m: (m, 0)),
                  row, row],
        out_specs=[pl.BlockSpec((tm, d), lambda m: (m, 0)), pl.BlockSpec((tm, d), lambda m: (m, 0))],
        out_shape=[jax.ShapeDtypeStruct((t, d), F32), jax.ShapeDtypeStruct((t, d), BF16)],
        compiler_params=_params("parallel"),
        name="ffn_down",
    )(act, w, x, g_post.reshape(1, d), g_next.reshape(1, d))


def kernel(x, rel_bias, ln_mix_pre, w_in, ssm_conv_w, ssm_conv_b, dt_bias, a_log, d_skip, ssm_norm_w, w_out,
           ln_mix_post, ln_ffn_pre, w_ffn_up, ffn_conv_w, ffn_conv_b, w_ffn_down, ln_ffn_post):
    bsz, seq, d = x.shape
    depth = w_in.shape[0]
    t = bsz * seq
    assert w_in.shape[2] == QKV_COLS + REST_COLS + SSM_HEADS

    rel_bias_t = rel_bias.T.astype(F32)
    town, tprev = _bias_tiles(rel_bias_t)
    w_in_t = jnp.swapaxes(w_in, 1, 2)
    w_dt_t = jnp.pad(w_in_t[:, QKV_COLS + REST_COLS:, :], ((0, 0), (0, LANES - SSM_HEADS), (0, 0)))
    ffn_conv_b3 = ffn_conv_b.reshape(depth, 1, -1)

    xf = x.reshape(t, d)
    h = _rmsnorm(xf, ln_mix_pre[0])
    for l in range(depth):
        qkv, rest, dt_raw, w_down_bf16 = _inproj(h, w_in_t, w_dt_t, w_ffn_down, l)
        attn = _moba(qkv.reshape(bsz, seq, QKV_COLS), rel_bias_t, town, tprev, bsz, seq)
        ssm = _ssd(rest.reshape(bsz, seq, REST_COLS), dt_raw.reshape(bsz, seq, LANES), ssm_conv_w[l],
                   ssm_conv_b[l], dt_bias[l], a_log[l], d_skip[l], ssm_norm_w[l], bsz, seq)
        xf, h = _outproj(attn.reshape(t, ATTN_WIDTH), ssm.reshape(t, SSM_WIDTH), w_out, l, xf,
                         ln_mix_post[l], ln_ffn_pre[l])
        act = _ffn_up(h, w_ffn_up, ffn_conv_w, ffn_conv_b3, l, seq)
        g_next = ln_mix_pre[(l + 1) % depth]
        xf, h = _ffn_down(act, w_down_bf16, xf, ln_ffn_post[l], g_next)
    return xf.reshape(bsz, seq, d)
```

```python
import functools
import math

import jax
import jax.numpy as jnp
import numpy as np
from jax import lax
from jax.experimental import pallas as pl
from jax.experimental.pallas import tpu as pltpu

F32 = jnp.float32
BF16 = jnp.bfloat16

LANES = 128
SUBLANES = 8
VMEM_LIMIT_BYTES = 56 * 1024 * 1024

D_MODEL = 2048
ATTN_HEAD_DIM = 128
ATTN_WIDTH = 1024
ATTN_HEADS = ATTN_WIDTH // ATTN_HEAD_DIM
MOBA_BLOCK = 256
MOBA_TOPK = 3
REL_BUCKETS = 32
REL_MAX_DIST = 128
SSM_WIDTH = 1024
SSM_HEAD_DIM = 64
SSM_HEADS = SSM_WIDTH // SSM_HEAD_DIM
SSM_GROUPS = 2
SSM_GROUP_WIDTH = SSM_WIDTH // SSM_GROUPS
SSM_STATE = 128
SSM_CONV = 4
SSM_CHUNK = 128
SSM_BC_WIDTH = 2 * SSM_GROUPS * SSM_STATE
FFN_HIDDEN = 5632
FFN_CONV = 3
NORM_EPS = 1e-6
NEG = -1e30
LOG2E = math.log2(math.e)

QKV_COLS = 3 * ATTN_WIDTH
REST_COLS = SSM_WIDTH + SSM_WIDTH + SSM_BC_WIDTH
CONV_HALO = SUBLANES


def _params(*semantics):
    return pltpu.CompilerParams(dimension_semantics=semantics, vmem_limit_bytes=VMEM_LIMIT_BYTES)


def _rms(x, w):
    return x * lax.rsqrt(jnp.mean(x * x, axis=-1, keepdims=True) + NORM_EPS) * w


def _split3(x):
    hi = x.astype(BF16)
    r1 = x - hi.astype(F32)
    mid = r1.astype(BF16)
    lo = (r1 - mid.astype(F32)).astype(BF16)
    return hi, mid, lo


def _dot(a, b):
    return jnp.dot(a, b, preferred_element_type=F32)


def _dot_nt(a, b):
    return lax.dot_general(a, b, (((1,), (1,)), ((), ())), preferred_element_type=F32)


def _rmsnorm_kernel(x_ref, w_ref, o_ref):
    o_ref[...] = _rms(x_ref[...], w_ref[...]).astype(o_ref.dtype)


def _rmsnorm(x, w, tm=512):
    t, d = x.shape
    return pl.pallas_call(
        _rmsnorm_kernel,
        grid=(t // tm,),
        in_specs=[pl.BlockSpec((tm, d), lambda m: (m, 0)),
                  pl.BlockSpec((1, d), lambda m: (0, 0))],
        out_specs=pl.BlockSpec((tm, d), lambda m: (m, 0)),
        out_shape=jax.ShapeDtypeStruct((t, d), BF16),
        compiler_params=_params("parallel"),
        name="rmsnorm",
    )(x, w.reshape(1, d))


IN_TN = 512
IN_QKV_TILES = QKV_COLS // IN_TN
IN_REST_TILES = REST_COLS // IN_TN


def _inproj_kernel(h_ref, wt_ref, wdt_ref, wdown_ref, qkv_ref, rest_ref, dt_ref, wdown_bf16_ref, wb_ref, wdtb_ref):
    m = pl.program_id(0)
    j = pl.program_id(1)

    @pl.when(m == 0)
    def _():
        wb_ref[j] = wt_ref[...].T.astype(BF16)

    @pl.when((m == 0) & (j == 0))
    def _():
        wdtb_ref[...] = wdt_ref[...].T.astype(BF16)

    def round_wdown_slice():
        wdown_bf16_ref[...] = wdown_ref[...].astype(BF16)

    @pl.when(j < IN_QKV_TILES)
    def _():
        qkv_ref[...] = _dot(h_ref[...], wb_ref[j]).astype(qkv_ref.dtype)
        round_wdown_slice()

    @pl.when(j >= IN_QKV_TILES)
    def _():
        rest_ref[...] = _dot(h_ref[...], wb_ref[j])
        round_wdown_slice()

    @pl.when(j == 0)
    def _():
        dt_ref[...] = _dot(h_ref[...], wdtb_ref[...])


def _inproj(h, w_in_t, w_dt_t, w_down, layer, tm=1024):
    t, d = h.shape
    n_tiles = IN_QKV_TILES + IN_REST_TILES
    n_steps = (t // tm) * n_tiles
    f_rows = w_down.shape[1] // n_steps
    assert f_rows * n_steps == w_down.shape[1] and f_rows % (2 * SUBLANES) == 0
    w_index = lambda m, j: (layer, jnp.where(m == 0, j, n_tiles - 1), 0)
    return pl.pallas_call(
        _inproj_kernel,
        grid=(t // tm, n_tiles),
        in_specs=[pl.BlockSpec((tm, d), lambda m, j: (m, 0)),
                  pl.BlockSpec((None, IN_TN, d), w_index),
                  pl.BlockSpec((None, LANES, d), lambda m, j: (layer, 0, 0)),
                  pl.BlockSpec((None, f_rows, d), lambda m, j: (layer, m * n_tiles + j, 0))],
        out_specs=[pl.BlockSpec((tm, IN_TN), lambda m, j: (m, jnp.minimum(j, IN_QKV_TILES - 1))),
                   pl.BlockSpec((tm, IN_TN), lambda m, j: (m, jnp.maximum(j - IN_QKV_TILES, 0))),
                   pl.BlockSpec((tm, LANES), lambda m, j: (m, 0)),
                   pl.BlockSpec((f_rows, d), lambda m, j: (m * n_tiles + j, 0))],
        out_shape=[jax.ShapeDtypeStruct((t, QKV_COLS), BF16),
                   jax.ShapeDtypeStruct((t, REST_COLS), F32),
                   jax.ShapeDtypeStruct((t, LANES), F32),
                   jax.ShapeDtypeStruct(w_down.shape[1:], BF16)],
        scratch_shapes=[pltpu.VMEM((n_tiles, d, IN_TN), BF16), pltpu.VMEM((d, LANES), BF16)],
        compiler_params=_params("arbitrary", "arbitrary"),
        name="in_proj",
    )(h, w_in_t, w_dt_t, w_down)


def _t5_bucket_np(rel):
    n = np.maximum(rel, 0)
    max_exact = REL_BUCKETS // 2
    nf = np.maximum(n, 1).astype(np.float32)
    large = max_exact + (np.log(nf / np.float32(max_exact)) / np.float32(math.log(REL_MAX_DIST / max_exact))
                         * np.float32(REL_BUCKETS - max_exact)).astype(np.int32)
    large = np.minimum(large, REL_BUCKETS - 1)
    return np.where(n < max_exact, n, large).astype(np.int32)


def _bias_tiles_kernel(rb_ref, bown_ref, bprev_ref, town_ref, tprev_ref):
    h = pl.program_id(0)
    bown = bown_ref[...]
    bprev = bprev_ref[...]
    town = jnp.full(bown.shape, NEG, F32)
    tprev = jnp.zeros(bprev.shape, F32)
    for b in range(REL_BUCKETS):
        val = rb_ref[h, b] * LOG2E
        town = jnp.where(bown == b, val, town)
        tprev = jnp.where(bprev == b, val, tprev)
    town_ref[...] = town
    tprev_ref[...] = tprev


def _bias_tiles(rel_bias_t):
    r = np.arange(MOBA_BLOCK)
    rel_own = r[None, :] - r[:, None]
    bown = np.where(rel_own >= 0, _t5_bucket_np(rel_own), -1).astype(np.int32)
    bprev = _t5_bucket_np(MOBA_BLOCK + rel_own)
    tile = pl.BlockSpec((MOBA_BLOCK, MOBA_BLOCK), lambda h: (0, 0))
    out_tile = pl.BlockSpec((None, MOBA_BLOCK, MOBA_BLOCK), lambda h: (h, 0, 0))
    shape = jax.ShapeDtypeStruct((ATTN_HEADS, MOBA_BLOCK, MOBA_BLOCK), F32)
    return pl.pallas_call(
        _bias_tiles_kernel,
        grid=(ATTN_HEADS,),
        in_specs=[pl.BlockSpec(memory_space=pltpu.SMEM), tile, tile],
        out_specs=[out_tile, out_tile],
        out_shape=[shape, shape],
        compiler_params=_params("parallel"),
        name="t5_bias_tiles",
    )(rel_bias_t, jnp.asarray(bown), jnp.asarray(bprev))


KM_ROWS = 2 * SUBLANES
MOBA_KEY_TILE = 256
MOBA_HEADS_PER_STEP = 2
ONES_ROWS = 2 * SUBLANES


def _moba_kernel(rb_ref, q_ref, k_ref, v_ref, town_ref, tprev_ref, o_ref, vt_ref, km_ref, s_ref):
    for hh in range(MOBA_HEADS_PER_STEP):
        _moba_head(hh, pl.program_id(1) * MOBA_HEADS_PER_STEP + hh, rb_ref, q_ref, k_ref, v_ref, town_ref, tprev_ref,
                   o_ref, vt_ref, km_ref, s_ref)


def _moba_head(hh, h, rb_ref, q_ref, k_ref, v_ref, town_ref, tprev_ref, o_ref, vt_ref, km_ref, s_ref):
    cols = slice(hh * ATTN_HEAD_DIM, (hh + 1) * ATTN_HEAD_DIM)
    nb = k_ref.shape[0] // MOBA_BLOCK
    scale = ATTN_HEAD_DIM ** -0.5 * LOG2E
    far_bias = rb_ref[h, REL_BUCKETS - 1] * LOG2E
    blocks = [slice(jb * MOBA_BLOCK, (jb + 1) * MOBA_BLOCK) for jb in range(nb)]
    tiles_per_block = MOBA_BLOCK // MOBA_KEY_TILE

    km_ref[hh] = jnp.zeros(km_ref.shape[1:], F32)
    vt_ref[hh, ATTN_HEAD_DIM:, :] = jnp.ones((ONES_ROWS, vt_ref.shape[2]), BF16)
    for jb in range(nb):
        vt_ref[hh, 0:ATTN_HEAD_DIM, blocks[jb]] = v_ref[blocks[jb], cols].astype(F32).T.astype(BF16)
        km_ref[hh, jb:jb + 1, :] = jnp.mean(k_ref[blocks[jb], cols].astype(F32), axis=0, keepdims=True)
    km = km_ref[hh]
    km_hi = km.astype(BF16)
    km_lo = (km - km_hi.astype(F32)).astype(BF16)

    for i in range(nb):
        q = q_ref[blocks[i], cols]
        gated = i > MOBA_TOPK
        if gated:
            gate_t = (_dot_nt(km_hi, q) + _dot_nt(km_lo, q))[0:SUBLANES, :]
            blk = lax.broadcasted_iota(jnp.int32, gate_t.shape, 0)
            rank = jnp.zeros(gate_t.shape, jnp.int32)
            for jp in range(i):
                row = gate_t[jp:jp + 1, :]
                beats = (row > gate_t) | ((row == gate_t) & (jp < blk))
                rank = rank + jnp.where(beats, 1, 0)
            mneg_t = jnp.where((rank < MOBA_TOPK) & (blk < i), 0.0, NEG)

        buf = i % 2
        key_tiles = [(j, kt) for j in range(i + 1) for kt in range(tiles_per_block)]
        m = None
        for n, (j, kt) in enumerate(key_tiles):
            in_blk = slice(kt * MOBA_KEY_TILE, (kt + 1) * MOBA_KEY_TILE)
            keys = slice(j * MOBA_BLOCK + in_blk.start, j * MOBA_BLOCK + in_blk.stop)
            s = _dot_nt(k_ref[keys, cols], q) * scale
            if j == i:
                s = s + town_ref[hh, in_blk, :]
            else:
                mrow = mneg_t[j:j + 1, :] if gated else None
                if j == i - 1:
                    s = s + tprev_ref[hh, in_blk, :]
                    if gated:
                        s = s + mrow
                else:
                    s = s + (far_bias + mrow if gated else far_bias)
            s_ref[hh, buf, n] = s
            cmax = jnp.max(s, axis=0, keepdims=True)
            m = cmax if m is None else jnp.maximum(m, cmax)

        acc = jnp.zeros((ATTN_HEAD_DIM + ONES_ROWS, MOBA_BLOCK), F32)
        for n, (j, kt) in enumerate(key_tiles):
            keys = slice(j * MOBA_BLOCK + kt * MOBA_KEY_TILE, j * MOBA_BLOCK + (kt + 1) * MOBA_KEY_TILE)
            p = jnp.exp2((s_ref[hh, buf, n] - m).astype(BF16))
            acc = acc + _dot(vt_ref[hh, :, keys], p)
        l = acc[ATTN_HEAD_DIM:ATTN_HEAD_DIM + 1, :]
        o_ref[blocks[i], cols] = (acc[0:ATTN_HEAD_DIM, :] / l).T.astype(o_ref.dtype)


def _moba(qkv, rel_bias_t, town, tprev, bsz, seq):
    nb = seq // MOBA_BLOCK
    hps = MOBA_HEADS_PER_STEP
    width = hps * ATTN_HEAD_DIM
    groups = ATTN_HEADS // hps
    assert (MOBA_BLOCK + MOBA_BLOCK) > REL_MAX_DIST and nb <= SUBLANES
    tile = pl.BlockSpec((hps, MOBA_BLOCK, MOBA_BLOCK), lambda b, g: (g, 0, 0))
    head_cols = lambda part: pl.BlockSpec((None, seq, width), lambda b, g: (b, 0, part * groups + g))
    return pl.pallas_call(
        _moba_kernel,
        grid=(bsz, groups),
        in_specs=[pl.BlockSpec(memory_space=pltpu.SMEM), head_cols(0), head_cols(1), head_cols(2), tile, tile],
        out_specs=pl.BlockSpec((None, seq, width), lambda b, g: (b, 0, g)),
        out_shape=jax.ShapeDtypeStruct((bsz, seq, ATTN_WIDTH), BF16),
        scratch_shapes=[pltpu.VMEM((hps, ATTN_HEAD_DIM + ONES_ROWS, seq), BF16),
                        pltpu.VMEM((hps, KM_ROWS, ATTN_HEAD_DIM), F32),
                        pltpu.VMEM((hps, 2, seq // MOBA_KEY_TILE, MOBA_KEY_TILE, MOBA_BLOCK), F32)],
        compiler_params=_params("parallel", "parallel"),
        name="moba_attention",
    )(rel_bias_t, qkv, qkv, qkv, town, tprev)


def _silu(x):
    return x * jax.nn.sigmoid(x)


def _ssd_kernel(z_ref, xs_ref, bc_ref, dt_ref, cwx_ref, cwbc_ref, cbx_ref, cbbc_ref, dtb_ref, alog_ref,
                dskip_ref, normw_ref, expand_ref, ltri_ref, o_ref, xbuf_ref, bcbuf_ref, st_ref, y_ref):
    L = SSM_CHUNK
    H = CONV_HALO
    rows_per_step = o_ref.shape[0]

    @pl.when(pl.program_id(1) == 0)
    def _():
        xbuf_ref[0:H, :] = jnp.zeros((H, xbuf_ref.shape[1]), F32)
        bcbuf_ref[0:H, :] = jnp.zeros((H, bcbuf_ref.shape[1]), F32)
        st_ref[...] = jnp.zeros(st_ref.shape, F32)

    xbuf_ref[H:H + rows_per_step, :] = xs_ref[...]
    bcbuf_ref[H:H + rows_per_step, :] = bc_ref[...]
    for r0 in range(0, rows_per_step, L):
        _ssd_chunk(r0, z_ref, dt_ref, cwx_ref, cwbc_ref, cbx_ref, cbbc_ref, dtb_ref, alog_ref, dskip_ref, normw_ref,
                   expand_ref, ltri_ref, o_ref, xbuf_ref, bcbuf_ref, st_ref, y_ref)
    xbuf_ref[0:H, :] = xbuf_ref[rows_per_step:rows_per_step + H, :]
    bcbuf_ref[0:H, :] = bcbuf_ref[rows_per_step:rows_per_step + H, :]


def _ssd_chunk(r0, z_ref, dt_ref, cwx_ref, cwbc_ref, cbx_ref, cbbc_ref, dtb_ref, alog_ref, dskip_ref, normw_ref,
               expand_ref, ltri_ref, o_ref, xbuf_ref, bcbuf_ref, st_ref, y_ref):
    L = SSM_CHUNK
    H = CONV_HALO
    rows = slice(r0, r0 + L)

    def conv_silu(buf_ref, w_ref, b_ref):
        u = buf_ref[r0:r0 + H + L, :]
        acc = w_ref[0:1, :] * u
        for k in range(1, SSM_CONV):
            acc = pltpu.roll(acc, 1, 0) + w_ref[k:k + 1, :] * u
        return _silu(acc[H:, :] + b_ref[...])

    xs = conv_silu(xbuf_ref, cwx_ref, cbx_ref)
    bc = conv_silu(bcbuf_ref, cwbc_ref, cbbc_ref)

    x_dt = dt_ref[rows, :] + dtb_ref[...]
    dt = jnp.maximum(x_dt, 0.0) + jnp.log1p(jnp.exp(-jnp.abs(x_dt)))
    da = dt * (-jnp.exp(alog_ref[...]) * LOG2E)
    ltri = ltri_ref[...]
    a_cs = sum(_dot(ltri, part) for part in _split3(da))
    a_last = a_cs[L - 1:L, :]
    decay_st = jnp.exp2(a_last - a_cs)
    exp_acs = jnp.exp2(a_cs)
    chunk_decay = jnp.broadcast_to(jnp.exp2(a_last), (2 * SUBLANES, LANES))

    stacked = jnp.concatenate([dt, decay_st, exp_acs, chunk_decay], axis=0)
    expand = expand_ref[...]
    wide = sum(_dot(part, expand) for part in _split3(stacked))
    dt_w = wide[0:L]
    decay_st_w = wide[L:2 * L]
    exp_acs_w = wide[2 * L:3 * L]
    chunk_decay_w = wide[3 * L:3 * L + 1]

    xc = xs * dt_w
    a_cs_t = a_cs.T
    row_i = lax.broadcasted_iota(jnp.int32, (L, L), 0)
    col_i = lax.broadcasted_iota(jnp.int32, (L, L), 1)
    causal = row_i >= col_i
    first_head = col_i < SSM_HEAD_DIM
    heads_per_group = SSM_HEADS // SSM_GROUPS

    for g in range(SSM_GROUPS):
        gsl = slice(g * SSM_GROUP_WIDTH, (g + 1) * SSM_GROUP_WIDTH)
        bg = bc[:, g * SSM_STATE:(g + 1) * SSM_STATE]
        cg = bc[:, (SSM_GROUPS + g) * SSM_STATE:(SSM_GROUPS + g + 1) * SSM_STATE].astype(BF16)
        cb = _dot_nt(cg, bg.astype(BF16))
        prev = st_ref[:, gsl]
        y_off = _dot(cg, prev.astype(BF16)) * exp_acs_w[:, gsl]
        st_ref[:, gsl] = prev * chunk_decay_w[:, gsl] + _dot(bg.T.astype(BF16),
                                                             (xc[:, gsl] * decay_st_w[:, gsl]).astype(BF16))
        for pr in range(heads_per_group // 2):
            h0 = g * heads_per_group + 2 * pr
            psl = slice(h0 * SSM_HEAD_DIM, (h0 + 2) * SSM_HEAD_DIM)
            xcp = xc[:, psl].astype(BF16)
            ys = []
            for hh in (h0, h0 + 1):
                seg = a_cs[:, hh:hh + 1] - a_cs_t[hh:hh + 1, :]
                decay = jnp.exp2(jnp.where(causal, seg, NEG))
                ys.append(_dot((cb * decay).astype(BF16), xcp))
            y_ref[rows, psl] = jnp.where(first_head, ys[0], ys[1]) + y_off[:, 2 * pr * SSM_HEAD_DIM:
                                                                            (2 * pr + 2) * SSM_HEAD_DIM]

    y = (y_ref[rows, :] + dskip_ref[...] * xs) * _silu(z_ref[rows, :])
    normw = normw_ref[...]
    for g in range(SSM_GROUPS):
        gsl = slice(g * SSM_GROUP_WIDTH, (g + 1) * SSM_GROUP_WIDTH)
        yg = y[:, gsl]
        yg = yg * lax.rsqrt(jnp.mean(yg * yg, axis=-1, keepdims=True) + NORM_EPS)
        o_ref[rows, gsl] = (yg * normw[:, gsl]).astype(o_ref.dtype)


def _ssd(rest, dt_raw, conv_w, conv_b, dt_bias, a_log, d_skip, norm_w, bsz, seq, chunks_per_step=4):
    L = SSM_CHUNK
    R = chunks_per_step * L
    pad_heads = LANES - SSM_HEADS
    expand = np.zeros((LANES, SSM_WIDTH), np.float32)
    for hh in range(SSM_HEADS):
        expand[hh, hh * SSM_HEAD_DIM:(hh + 1) * SSM_HEAD_DIM] = 1.0
    ltri = np.tril(np.ones((L, L), np.float32))

    row = lambda v: v.reshape(1, -1).astype(F32)
    const = lambda shape: pl.BlockSpec(shape, lambda b, c: (0, 0))
    args = (rest, rest, rest, dt_raw,
            conv_w[:, :SSM_WIDTH], conv_w[:, SSM_WIDTH:], row(conv_b[:SSM_WIDTH]), row(conv_b[SSM_WIDTH:]),
            row(jnp.pad(dt_bias, (0, pad_heads))), row(jnp.pad(a_log, (0, pad_heads))),
            row(jnp.repeat(d_skip, SSM_HEAD_DIM)), row(norm_w),
            jnp.asarray(expand, BF16), jnp.asarray(ltri, BF16))
    in_specs = [
        pl.BlockSpec((None, R, SSM_WIDTH), lambda b, c: (b, c, 0)),
        pl.BlockSpec((None, R, SSM_WIDTH), lambda b, c: (b, c, 1)),
        pl.BlockSpec((None, R, SSM_BC_WIDTH), lambda b, c: (b, c, 2 * SSM_WIDTH // SSM_BC_WIDTH)),
        pl.BlockSpec((None, R, LANES), lambda b, c: (b, c, 0)),
        const((SSM_CONV, SSM_WIDTH)), const((SSM_CONV, SSM_BC_WIDTH)),
        const((1, SSM_WIDTH)), const((1, SSM_BC_WIDTH)),
        const((1, LANES)), const((1, LANES)), const((1, SSM_WIDTH)), const((1, SSM_WIDTH)),
        const((LANES, SSM_WIDTH)), const((L, L)),
    ]
    return pl.pallas_call(
        _ssd_kernel,
        grid=(bsz, seq // R),
        in_specs=in_specs,
        out_specs=pl.BlockSpec((None, R, SSM_WIDTH), lambda b, c: (b, c, 0)),
        out_shape=jax.ShapeDtypeStruct((bsz, seq, SSM_WIDTH), BF16),
        scratch_shapes=[pltpu.VMEM((R + CONV_HALO, SSM_WIDTH), F32),
                        pltpu.VMEM((R + CONV_HALO, SSM_BC_WIDTH), F32),
                        pltpu.VMEM((SSM_STATE, SSM_WIDTH), F32),
                        pltpu.VMEM((R, SSM_WIDTH), F32)],
        compiler_params=_params("parallel", "arbitrary"),
        name="ssd_mixer",
    )(*args)


def _outproj_kernel(attn_ref, ssm_ref, w_ref, x_ref, gpost_ref, gnext_ref, xo_ref, ho_ref, wb_ref, *, sub):
    @pl.when(pl.program_id(0) == 0)
    def _():
        wb_ref[...] = w_ref[...].astype(BF16)

    for r0 in range(0, x_ref.shape[0], sub):
        rows = slice(r0, r0 + sub)
        mixed = _dot(jnp.concatenate([attn_ref[rows, :], ssm_ref[rows, :]], axis=1), wb_ref[...])
        x_new = x_ref[rows, :] + _rms(mixed, gpost_ref[...])
        xo_ref[rows, :] = x_new
        ho_ref[rows, :] = _rms(x_new, gnext_ref[...]).astype(ho_ref.dtype)


def _outproj(attn, ssm, w, layer, x, g_post, g_next, tm=512, sub=256):
    t, d = x.shape
    row = pl.BlockSpec((1, d), lambda m: (0, 0))
    return pl.pallas_call(
        functools.partial(_outproj_kernel, sub=sub),
        grid=(t // tm,),
        in_specs=[pl.BlockSpec((tm, ATTN_WIDTH), lambda m: (m, 0)),
                  pl.BlockSpec((tm, SSM_WIDTH), lambda m: (m, 0)),
                  pl.BlockSpec((None,) + w.shape[1:], lambda m: (layer, 0, 0), pipeline_mode=pl.Buffered(1)),
                  pl.BlockSpec((tm, d), lambda m: (m, 0)),
                  row, row],
        out_specs=[pl.BlockSpec((tm, d), lambda m: (m, 0)), pl.BlockSpec((tm, d), lambda m: (m, 0))],
        out_shape=[jax.ShapeDtypeStruct((t, d), F32), jax.ShapeDtypeStruct((t, d), BF16)],
        scratch_shapes=[pltpu.VMEM(w.shape[1:], BF16)],
        compiler_params=_params("arbitrary"),
        name="out_proj",
    )(attn, ssm, w, x, g_post.reshape(1, d), g_next.reshape(1, d))


def _ffn_up_kernel(h_ref, wg_ref, wu_ref, cwg_ref, cwu_ref, cbg_ref, cbu_ref, o_ref, wb_ref, *, row_tiles):
    assert sum(row_tiles) == h_ref.shape[0]
    H = CONV_HALO
    tn = o_ref.shape[1]

    @pl.when(pl.program_id(1) == 0)
    def _():
        wb_ref[:, 0:tn] = wg_ref[...].astype(BF16)
        wb_ref[:, tn:] = wu_ref[...].astype(BF16)

    def conv(u, cw_ref, cb_ref):
        acc = cw_ref[0:1, :] * u
        for k in range(1, FFN_CONV):
            acc = pltpu.roll(acc, 1, 0) + cw_ref[k:k + 1, :] * u
        return acc[H:, :] + cb_ref[...]

    r0 = 0
    tail = jnp.zeros((H, 2 * tn), F32)
    for tm in row_tiles:
        res = _dot(h_ref[r0:r0 + tm, :], wb_ref[...])
        u = jnp.concatenate([tail, res], axis=0)
        gate = conv(u[:, 0:tn], cwg_ref, cbg_ref)
        up = conv(u[:, tn:], cwu_ref, cbu_ref)
        o_ref[r0:r0 + tm, :] = (jax.nn.gelu(gate, approximate=True) * up).astype(o_ref.dtype)
        tail = res[tm - H:, :]
        r0 += tm


def _ffn_up(h, w_up, conv_w, conv_b, layer, seq, row_tiles=(1024, 1024), tn=512):
    t, d = h.shape
    nt = FFN_HIDDEN // tn
    col = lambda rows, half: pl.BlockSpec((None, rows, tn), lambda j, b: (layer, 0, half * nt + j))
    return pl.pallas_call(
        functools.partial(_ffn_up_kernel, row_tiles=row_tiles),
        grid=(nt, t // seq),
        in_specs=[pl.BlockSpec((seq, d), lambda j, b: (b, 0)),
                  col(d, 0), col(d, 1), col(FFN_CONV, 0), col(FFN_CONV, 1), col(1, 0), col(1, 1)],
        out_specs=pl.BlockSpec((seq, tn), lambda j, b: (b, j)),
        out_shape=jax.ShapeDtypeStruct((t, FFN_HIDDEN), BF16),
        scratch_shapes=[pltpu.VMEM((d, 2 * tn), BF16)],
        compiler_params=_params("parallel", "arbitrary"),
        name="ffn_up_conv_geglu",
    )(h, w_up, w_up, conv_w, conv_w, conv_b, conv_b)


def _ffn_down_kernel(a_ref, w_ref, x_ref, gpost_ref, gnext_ref, xo_ref, ho_ref, *, sub):
    for r0 in range(0, x_ref.shape[0], sub):
        rows = slice(r0, r0 + sub)
        x_new = x_ref[rows, :] + _rms(_dot(a_ref[rows, :], w_ref[...]), gpost_ref[...])
        xo_ref[rows, :] = x_new
        ho_ref[rows, :] = _rms(x_new, gnext_ref[...]).astype(ho_ref.dtype)


def _ffn_down(act, w, x, g_post, g_next, tm=512, sub=256):
    t, d = x.shape
    kdim = act.shape[1]
    row = pl.BlockSpec((1, d), lambda m: (0, 0))
    return pl.pallas_call(
        functools.partial(_ffn_down_kernel, sub=sub),
        grid=(t // tm,),
        in_specs=[pl.BlockSpec((tm, kdim), lambda m: (m, 0)),
                  pl.BlockSpec((kdim, d), lambda m: (0, 0), pipeline_mode=pl.Buffered(1)),
                  pl.BlockSpec((tm, d), lambda m: (m, 0)),
                  row, row],
        out_specs=[pl.BlockSpec((tm, d), lambda m: (m, 0)), pl.BlockSpec((tm, d), lambda m: (m, 0))],
        out_shape=[jax.ShapeDtypeStruct((t, d), F32), jax.ShapeDtypeStruct((t, d), BF16)],
        compiler_params=_params("parallel"),
        name="ffn_down",
    )(act, w, x, g_post.reshape(1, d), g_next.reshape(1, d))


def kernel(x, rel_bias, ln_mix_pre, w_in, ssm_conv_w, ssm_conv_b, dt_bias, a_log, d_skip, ssm_norm_w, w_out,
           ln_mix_post, ln_ffn_pre, w_ffn_up, ffn_conv_w, ffn_conv_b, w_ffn_down, ln_ffn_post):
    bsz, seq, d = x.shape
    depth = w_in.shape[0]
    t = bsz * seq
    assert w_in.shape[2] == QKV_COLS + REST_COLS + SSM_HEADS

    rel_bias_t = rel_bias.T.astype(F32)
    town, tprev = _bias_tiles(rel_bias_t)
    w_in_t = jnp.swapaxes(w_in, 1, 2)
    w_dt_t = jnp.pad(w_in_t[:, QKV_COLS + REST_COLS:, :], ((0, 0), (0, LANES - SSM_HEADS), (0, 0)))
    ffn_conv_b3 = ffn_conv_b.reshape(depth, 1, -1)

    xf = x.reshape(t, d)
    h = _rmsnorm(xf, ln_mix_pre[0])
    for l in range(depth):
        qkv, rest, dt_raw, w_down_bf16 = _inproj(h, w_in_t, w_dt_t, w_ffn_down, l)
        attn = _moba(qkv.reshape(bsz, seq, QKV_COLS), rel_bias_t, town, tprev, bsz, seq)
        ssm = _ssd(rest.reshape(bsz, seq, REST_COLS), dt_raw.reshape(bsz, seq, LANES), ssm_conv_w[l],
                   ssm_conv_b[l], dt_bias[l], a_log[l], d_skip[l], ssm_norm_w[l], bsz, seq)
        xf, h = _outproj(attn.reshape(t, ATTN_WIDTH), ssm.reshape(t, SSM_WIDTH), w_out, l, xf,
                         ln_mix_post[l], ln_ffn_pre[l])
        act = _ffn_up(h, w_ffn_up, ffn_conv_w, ffn_conv_b3, l, seq)
        g_next = ln_mix_pre[(l + 1) % depth]
        xf, h = _ffn_down(act, w_down_bf16, xf, ln_ffn_post[l], g_next)
    return xf.reshape(bsz, seq, d)
```

```python
import functools
import math

import jax
import jax.numpy as jnp
import numpy as np
from jax import lax
from jax.experimental import pallas as pl
from jax.experimental.pallas import tpu as pltpu

F32 = jnp.float32
BF16 = jnp.bfloat16

LANES = 128
SUBLANES = 8
VMEM_LIMIT_BYTES = 56 * 1024 * 1024

D_MODEL = 2048
ATTN_HEAD_DIM = 128
ATTN_WIDTH = 1024
ATTN_HEADS = ATTN_WIDTH // ATTN_HEAD_DIM
MOBA_BLOCK = 256
MOBA_TOPK = 3
REL_BUCKETS = 32
REL_MAX_DIST = 128
SSM_WIDTH = 1024
SSM_HEAD_DIM = 64
SSM_HEADS = SSM_WIDTH // SSM_HEAD_DIM
SSM_GROUPS = 2
SSM_GROUP_WIDTH = SSM_WIDTH // SSM_GROUPS
SSM_STATE = 128
SSM_CONV = 4
SSM_CHUNK = 128
SSM_BC_WIDTH = 2 * SSM_GROUPS * SSM_STATE
FFN_HIDDEN = 5632
FFN_CONV = 3
NORM_EPS = 1e-6
NEG = -1e30
LOG2E = math.log2(math.e)

QKV_COLS = 3 * ATTN_WIDTH
REST_COLS = SSM_WIDTH + SSM_WIDTH + SSM_BC_WIDTH
CONV_HALO = SUBLANES


def _params(*semantics):
    return pltpu.CompilerParams(dimension_semantics=semantics, vmem_limit_bytes=VMEM_LIMIT_BYTES)


def _rms(x, w):
    return x * lax.rsqrt(jnp.mean(x * x, axis=-1, keepdims=True) + NORM_EPS) * w


def _split3(x):
    hi = x.astype(BF16)
    r1 = x - hi.astype(F32)
    mid = r1.astype(BF16)
    lo = (r1 - mid.astype(F32)).astype(BF16)
    return hi, mid, lo


def _dot(a, b):
    return jnp.dot(a, b, preferred_element_type=F32)


def _dot_nt(a, b):
    return lax.dot_general(a, b, (((1,), (1,)), ((), ())), preferred_element_type=F32)


def _rmsnorm_kernel(x_ref, w_ref, o_ref):
    o_ref[...] = _rms(x_ref[...], w_ref[...]).astype(o_ref.dtype)


def _rmsnorm(x, w, tm=512):
    t, d = x.shape
    return pl.pallas_call(
        _rmsnorm_kernel,
        grid=(t // tm,),
        in_specs=[pl.BlockSpec((tm, d), lambda m: (m, 0)),
                  pl.BlockSpec((1, d), lambda m: (0, 0))],
        out_specs=pl.BlockSpec((tm, d), lambda m: (m, 0)),
        out_shape=jax.ShapeDtypeStruct((t, d), BF16),
        compiler_params=_params("parallel"),
        name="rmsnorm",
    )(x, w.reshape(1, d))


IN_TN = 512
IN_QKV_TILES = QKV_COLS // IN_TN
IN_REST_TILES = REST_COLS // IN_TN


def _inproj_kernel(h_ref, wt_ref, wdt_ref, wdown_ref, qkv_ref, rest_ref, dt_ref, wdown_bf16_ref, wb_ref, wdtb_ref):
    m = pl.program_id(0)
    j = pl.program_id(1)

    @pl.when(m == 0)
    def _():
        wb_ref[j] = wt_ref[...].T.astype(BF16)

    @pl.when((m == 0) & (j == 0))
    def _():
        wdtb_ref[...] = wdt_ref[...].T.astype(BF16)

    def round_wdown_slice():
        wdown_bf16_ref[...] = wdown_ref[...].astype(BF16)

    @pl.when(j < IN_QKV_TILES)
    def _():
        qkv_ref[...] = _dot(h_ref[...], wb_ref[j]).astype(qkv_ref.dtype)
        round_wdown_slice()

    @pl.when(j >= IN_QKV_TILES)
    def _():
        rest_ref[...] = _dot(h_ref[...], wb_ref[j])
        round_wdown_slice()

    @pl.when(j == 0)
    def _():
        dt_ref[...] = _dot(h_ref[...], wdtb_ref[...])


def _inproj(h, w_in_t, w_dt_t, w_down, layer, tm=1024):
    t, d = h.shape
    n_tiles = IN_QKV_TILES + IN_REST_TILES
    n_steps = (t // tm) * n_tiles
    f_rows = w_down.shape[1] // n_steps
    assert f_rows * n_steps == w_down.shape[1] and f_rows % (2 * SUBLANES) == 0
    w_index = lambda m, j: (layer, jnp.where(m == 0, j, n_tiles - 1), 0)
    return pl.pallas_call(
        _inproj_kernel,
        grid=(t // tm, n_tiles),
        in_specs=[pl.BlockSpec((tm, d), lambda m, j: (m, 0)),
                  pl.BlockSpec((None, IN_TN, d), w_index),
                  pl.BlockSpec((None, LANES, d), lambda m, j: (layer, 0, 0)),
                  pl.BlockSpec((None, f_rows, d), lambda m, j: (layer, m * n_tiles + j, 0))],
        out_specs=[pl.BlockSpec((tm, IN_TN), lambda m, j: (m, jnp.minimum(j, IN_QKV_TILES - 1))),
                   pl.BlockSpec((tm, IN_TN), lambda m, j: (m, jnp.maximum(j - IN_QKV_TILES, 0))),
                   pl.BlockSpec((tm, LANES), lambda m, j: (m, 0)),
                   pl.BlockSpec((f_rows, d), lambda m, j: (m * n_tiles + j, 0))],
        out_shape=[jax.ShapeDtypeStruct((t, QKV_COLS), BF16),
                   jax.ShapeDtypeStruct((t, REST_COLS), F32),
                   jax.ShapeDtypeStruct((t, LANES), F32),
                   jax.ShapeDtypeStruct(w_down.shape[1:], BF16)],
        scratch_shapes=[pltpu.VMEM((n_tiles, d, IN_TN), BF16), pltpu.VMEM((d, LANES), BF16)],
        compiler_params=_params("arbitrary", "arbitrary"),
        name="in_proj",
    )(h, w_in_t, w_dt_t, w_down)


def _t5_bucket_np(rel):
    n = np.maximum(rel, 0)
    max_exact = REL_BUCKETS // 2
    nf = np.maximum(n, 1).astype(np.float32)
    large = max_exact + (np.log(nf / np.float32(max_exact)) / np.float32(math.log(REL_MAX_DIST / max_exact))
                         * np.float32(REL_BUCKETS - max_exact)).astype(np.int32)
    large = np.minimum(large, REL_BUCKETS - 1)
    return np.where(n < max_exact, n, large).astype(np.int32)


def _bias_tiles_kernel(rb_ref, bown_ref, bprev_ref, town_ref, tprev_ref):
    h = pl.program_id(0)
    bown = bown_ref[...]
    bprev = bprev_ref[...]
    town = jnp.full(bown.shape, NEG, F32)
    tprev = jnp.zeros(bprev.shape, F32)
    for b in range(REL_BUCKETS):
        val = rb_ref[h, b] * LOG2E
        town = jnp.where(bown == b, val, town)
        tprev = jnp.where(bprev == b, val, tprev)
    town_ref[...] = town
    tprev_ref[...] = tprev


def _bias_tiles(rel_bias_t):
    r = np.arange(MOBA_BLOCK)
    rel_own = r[None, :] - r[:, None]
    bown = np.where(rel_own >= 0, _t5_bucket_np(rel_own), -1).astype(np.int32)
    bprev = _t5_bucket_np(MOBA_BLOCK + rel_own)
    tile = pl.BlockSpec((MOBA_BLOCK, MOBA_BLOCK), lambda h: (0, 0))
    out_tile = pl.BlockSpec((None, MOBA_BLOCK, MOBA_BLOCK), lambda h: (h, 0, 0))
    shape = jax.ShapeDtypeStruct((ATTN_HEADS, MOBA_BLOCK, MOBA_BLOCK), F32)
    return pl.pallas_call(
        _bias_tiles_kernel,
        grid=(ATTN_HEADS,),
        in_specs=[pl.BlockSpec(memory_space=pltpu.SMEM), tile, tile],
        out_specs=[out_tile, out_tile],
        out_shape=[shape, shape],
        compiler_params=_params("parallel"),
        name="t5_bias_tiles",
    )(rel_bias_t, jnp.asarray(bown), jnp.asarray(bprev))


KM_ROWS = 2 * SUBLANES
MOBA_KEY_TILE = 256
MOBA_HEADS_PER_STEP = 4
ONES_ROWS = 2 * SUBLANES


def _moba_kernel(rb_ref, q_ref, k_ref, v_ref, town_ref, tprev_ref, o_ref, vt_ref, km_ref, s_ref):
    for hh in range(MOBA_HEADS_PER_STEP):
        _moba_head(hh, pl.program_id(1) * MOBA_HEADS_PER_STEP + hh, rb_ref, q_ref, k_ref, v_ref, town_ref, tprev_ref,
                   o_ref, vt_ref, km_ref, s_ref)


def _moba_head(hh, h, rb_ref, q_ref, k_ref, v_ref, town_ref, tprev_ref, o_ref, vt_ref, km_ref, s_ref):
    cols = slice(hh * ATTN_HEAD_DIM, (hh + 1) * ATTN_HEAD_DIM)
    nb = k_ref.shape[0] // MOBA_BLOCK
    scale = ATTN_HEAD_DIM ** -0.5 * LOG2E
    far_bias = rb_ref[h, REL_BUCKETS - 1] * LOG2E
    blocks = [slice(jb * MOBA_BLOCK, (jb + 1) * MOBA_BLOCK) for jb in range(nb)]
    tiles_per_block = MOBA_BLOCK // MOBA_KEY_TILE

    km_ref[hh] = jnp.zeros(km_ref.shape[1:], F32)
    vt_ref[hh, ATTN_HEAD_DIM:, :] = jnp.ones((ONES_ROWS, vt_ref.shape[2]), BF16)
    for jb in range(nb):
        vt_ref[hh, 0:ATTN_HEAD_DIM, blocks[jb]] = v_ref[blocks[jb], cols].astype(F32).T.astype(BF16)
        km_ref[hh, jb:jb + 1, :] = jnp.mean(k_ref[blocks[jb], cols].astype(F32), axis=0, keepdims=True)
    km = km_ref[hh]
    km_hi = km.astype(BF16)
    km_lo = (km - km_hi.astype(F32)).astype(BF16)

    for i in range(nb):
        q = q_ref[blocks[i], cols]
        gated = i > MOBA_TOPK
        if gated:
            gate_t = (_dot_nt(km_hi, q) + _dot_nt(km_lo, q))[0:SUBLANES, :]
            blk = lax.broadcasted_iota(jnp.int32, gate_t.shape, 0)
            rank = jnp.zeros(gate_t.shape, jnp.int32)
            for jp in range(i):
                row = gate_t[jp:jp + 1, :]
                beats = (row > gate_t) | ((row == gate_t) & (jp < blk))
                rank = rank + jnp.where(beats, 1, 0)
            mneg_t = jnp.where((rank < MOBA_TOPK) & (blk < i), 0.0, NEG)

        buf = i % 2
        key_tiles = [(j, kt) for j in range(i + 1) for kt in range(tiles_per_block)]
        m = None
        for n, (j, kt) in enumerate(key_tiles):
            in_blk = slice(kt * MOBA_KEY_TILE, (kt + 1) * MOBA_KEY_TILE)
            keys = slice(j * MOBA_BLOCK + in_blk.start, j * MOBA_BLOCK + in_blk.stop)
            s = _dot_nt(k_ref[keys, cols], q) * scale
            if j == i:
                s = s + town_ref[hh, in_blk, :]
            else:
                mrow = mneg_t[j:j + 1, :] if gated else None
                if j == i - 1:
                    s = s + tprev_ref[hh, in_blk, :]
                    if gated:
                        s = s + mrow
                else:
                    s = s + (far_bias + mrow if gated else far_bias)
            s_ref[hh, buf, n] = s
            cmax = jnp.max(s, axis=0, keepdims=True)
            m = cmax if m is None else jnp.maximum(m, cmax)

        acc = jnp.zeros((ATTN_HEAD_DIM + ONES_ROWS, MOBA_BLOCK), F32)
        for n, (j, kt) in enumerate(key_tiles):
            keys = slice(j * MOBA_BLOCK + kt * MOBA_KEY_TILE, j * MOBA_BLOCK + (kt + 1) * MOBA_KEY_TILE)
            p = jnp.exp2((s_ref[hh, buf, n] - m).astype(BF16))
            acc = acc + _dot(vt_ref[hh, :, keys], p)
        l = acc[ATTN_HEAD_DIM:ATTN_HEAD_DIM + 1, :]
        o_ref[blocks[i], cols] = (acc[0:ATTN_HEAD_DIM, :] / l).T.astype(o_ref.dtype)


def _moba(qkv, rel_bias_t, town, tprev, bsz, seq):
    nb = seq // MOBA_BLOCK
    hps = MOBA_HEADS_PER_STEP
    width = hps * ATTN_HEAD_DIM
    groups = ATTN_HEADS // hps
    assert (MOBA_BLOCK + MOBA_BLOCK) > REL_MAX_DIST and nb <= SUBLANES
    tile = pl.BlockSpec((hps, MOBA_BLOCK, MOBA_BLOCK), lambda b, g: (g, 0, 0))
    head_cols = lambda part: pl.BlockSpec((None, seq, width), lambda b, g: (b, 0, part * groups + g))
    return pl.pallas_call(
        _moba_kernel,
        grid=(bsz, groups),
        in_specs=[pl.BlockSpec(memory_space=pltpu.SMEM), head_cols(0), head_cols(1), head_cols(2), tile, tile],
        out_specs=pl.BlockSpec((None, seq, width), lambda b, g: (b, 0, g)),
        out_shape=jax.ShapeDtypeStruct((bsz, seq, ATTN_WIDTH), BF16),
        scratch_shapes=[pltpu.VMEM((hps, ATTN_HEAD_DIM + ONES_ROWS, seq), BF16),
                        pltpu.VMEM((hps, KM_ROWS, ATTN_HEAD_DIM), F32),
                        pltpu.VMEM((hps, 2, seq // MOBA_KEY_TILE, MOBA_KEY_TILE, MOBA_BLOCK), F32)],
        compiler_params=_params("parallel", "parallel"),
        name="moba_attention",
    )(rel_bias_t, qkv, qkv, qkv, town, tprev)


def _silu(x):
    return x * jax.nn.sigmoid(x)


def _ssd_kernel(z_ref, xs_ref, bc_ref, dt_ref, cwx_ref, cwbc_ref, cbx_ref, cbbc_ref, dtb_ref, alog_ref,
                dskip_ref, normw_ref, expand_ref, ltri_ref, o_ref, xbuf_ref, bcbuf_ref, st_ref, y_ref):
    L = SSM_CHUNK
    H = CONV_HALO
    rows_per_step = o_ref.shape[0]

    @pl.when(pl.program_id(1) == 0)
    def _():
        xbuf_ref[0:H, :] = jnp.zeros((H, xbuf_ref.shape[1]), F32)
        bcbuf_ref[0:H, :] = jnp.zeros((H, bcbuf_ref.shape[1]), F32)
        st_ref[...] = jnp.zeros(st_ref.shape, F32)

    xbuf_ref[H:H + rows_per_step, :] = xs_ref[...]
    bcbuf_ref[H:H + rows_per_step, :] = bc_ref[...]
    for r0 in range(0, rows_per_step, L):
        _ssd_chunk(r0, z_ref, dt_ref, cwx_ref, cwbc_ref, cbx_ref, cbbc_ref, dtb_ref, alog_ref, dskip_ref, normw_ref,
                   expand_ref, ltri_ref, o_ref, xbuf_ref, bcbuf_ref, st_ref, y_ref)
    xbuf_ref[0:H, :] = xbuf_ref[rows_per_step:rows_per_step + H, :]
    bcbuf_ref[0:H, :] = bcbuf_ref[rows_per_step:rows_per_step + H, :]


def _ssd_chunk(r0, z_ref, dt_ref, cwx_ref, cwbc_ref, cbx_ref, cbbc_ref, dtb_ref, alog_ref, dskip_ref, normw_ref,
               expand_ref, ltri_ref, o_ref, xbuf_ref, bcbuf_ref, st_ref, y_ref):
    L = SSM_CHUNK
    H = CONV_HALO
    rows = slice(r0, r0 + L)

    def conv_silu(buf_ref, w_ref, b_ref):
        u = buf_ref[r0:r0 + H + L, :]
        acc = w_ref[0:1, :] * u
        for k in range(1, SSM_CONV):
            acc = pltpu.roll(acc, 1, 0) + w_ref[k:k + 1, :] * u
        return _silu(acc[H:, :] + b_ref[...])

    xs = conv_silu(xbuf_ref, cwx_ref, cbx_ref)
    bc = conv_silu(bcbuf_ref, cwbc_ref, cbbc_ref)

    x_dt = dt_ref[rows, :] + dtb_ref[...]
    dt = jnp.maximum(x_dt, 0.0) + jnp.log1p(jnp.exp(-jnp.abs(x_dt)))
    da = dt * (-jnp.exp(alog_ref[...]) * LOG2E)
    ltri = ltri_ref[...]
    a_cs = sum(_dot(ltri, part) for part in _split3(da))
    a_last = a_cs[L - 1:L, :]
    decay_st = jnp.exp2(a_last - a_cs)
    exp_acs = jnp.exp2(a_cs)
    chunk_decay = jnp.broadcast_to(jnp.exp2(a_last), (2 * SUBLANES, LANES))

    stacked = jnp.concatenate([dt, decay_st, exp_acs, chunk_decay], axis=0)
    expand = expand_ref[...]
    wide = sum(_dot(part, expand) for part in _split3(stacked))
    dt_w = wide[0:L]
    decay_st_w = wide[L:2 * L]
    exp_acs_w = wide[2 * L:3 * L]
    chunk_decay_w = wide[3 * L:3 * L + 1]

    xc = xs * dt_w
    a_cs_t = a_cs.T
    row_i = lax.broadcasted_iota(jnp.int32, (L, L), 0)
    col_i = lax.broadcasted_iota(jnp.int32, (L, L), 1)
    causal = row_i >= col_i
    first_head = col_i < SSM_HEAD_DIM
    heads_per_group = SSM_HEADS // SSM_GROUPS

    for g in range(SSM_GROUPS):
        gsl = slice(g * SSM_GROUP_WIDTH, (g + 1) * SSM_GROUP_WIDTH)
        bg = bc[:, g * SSM_STATE:(g + 1) * SSM_STATE]
        cg = bc[:, (SSM_GROUPS + g) * SSM_STATE:(SSM_GROUPS + g + 1) * SSM_STATE].astype(BF16)
        cb = _dot_nt(cg, bg.astype(BF16))
        prev = st_ref[:, gsl]
        y_off = _dot(cg, prev.astype(BF16)) * exp_acs_w[:, gsl]
        st_ref[:, gsl] = prev * chunk_decay_w[:, gsl] + _dot(bg.T.astype(BF16),
                                                             (xc[:, gsl] * decay_st_w[:, gsl]).astype(BF16))
        for pr in range(heads_per_group // 2):
            h0 = g * heads_per_group + 2 * pr
            psl = slice(h0 * SSM_HEAD_DIM, (h0 + 2) * SSM_HEAD_DIM)
            xcp = xc[:, psl].astype(BF16)
            ys = []
            for hh in (h0, h0 + 1):
                seg = a_cs[:, hh:hh + 1] - a_cs_t[hh:hh + 1, :]
                decay = jnp.exp2(jnp.where(causal, seg, NEG))
                ys.append(_dot((cb * decay).astype(BF16), xcp))
            y_ref[rows, psl] = jnp.where(first_head, ys[0], ys[1]) + y_off[:, 2 * pr * SSM_HEAD_DIM:
                                                                            (2 * pr + 2) * SSM_HEAD_DIM]

    y = (y_ref[rows, :] + dskip_ref[...] * xs) * _silu(z_ref[rows, :])
    normw = normw_ref[...]
    for g in range(SSM_GROUPS):
        gsl = slice(g * SSM_GROUP_WIDTH, (g + 1) * SSM_GROUP_WIDTH)
        yg = y[:, gsl]
        yg = yg * lax.rsqrt(jnp.mean(yg * yg, axis=-1, keepdims=True) + NORM_EPS)
        o_ref[rows, gsl] = (yg * normw[:, gsl]).astype(o_ref.dtype)


def _ssd(rest, dt_raw, conv_w, conv_b, dt_bias, a_log, d_skip, norm_w, bsz, seq, chunks_per_step=4):
    L = SSM_CHUNK
    R = chunks_per_step * L
    pad_heads = LANES - SSM_HEADS
    expand = np.zeros((LANES, SSM_WIDTH), np.float32)
    for hh in range(SSM_HEADS):
        expand[hh, hh * SSM_HEAD_DIM:(hh + 1) * SSM_HEAD_DIM] = 1.0
    ltri = np.tril(np.ones((L, L), np.float32))

    row = lambda v: v.reshape(1, -1).astype(F32)
    const = lambda shape: pl.BlockSpec(shape, lambda b, c: (0, 0))
    args = (rest, rest, rest, dt_raw,
            conv_w[:, :SSM_WIDTH], conv_w[:, SSM_WIDTH:], row(conv_b[:SSM_WIDTH]), row(conv_b[SSM_WIDTH:]),
            row(jnp.pad(dt_bias, (0, pad_heads))), row(jnp.pad(a_log, (0, pad_heads))),
            row(jnp.repeat(d_skip, SSM_HEAD_DIM)), row(norm_w),
            jnp.asarray(expand, BF16), jnp.asarray(ltri, BF16))
    in_specs = [
        pl.BlockSpec((None, R, SSM_WIDTH), lambda b, c: (b, c, 0)),
        pl.BlockSpec((None, R, SSM_WIDTH), lambda b, c: (b, c, 1)),
        pl.BlockSpec((None, R, SSM_BC_WIDTH), lambda b, c: (b, c, 2 * SSM_WIDTH // SSM_BC_WIDTH)),
        pl.BlockSpec((None, R, LANES), lambda b, c: (b, c, 0)),
        const((SSM_CONV, SSM_WIDTH)), const((SSM_CONV, SSM_BC_WIDTH)),
        const((1, SSM_WIDTH)), const((1, SSM_BC_WIDTH)),
        const((1, LANES)), const((1, LANES)), const((1, SSM_WIDTH)), const((1, SSM_WIDTH)),
        const((LANES, SSM_WIDTH)), const((L, L)),
    ]
    return pl.pallas_call(
        _ssd_kernel,
        grid=(bsz, seq // R),
        in_specs=in_specs,
        out_specs=pl.BlockSpec((None, R, SSM_WIDTH), lambda b, c: (b, c, 0)),
        out_shape=jax.ShapeDtypeStruct((bsz, seq, SSM_WIDTH), BF16),
        scratch_shapes=[pltpu.VMEM((R + CONV_HALO, SSM_WIDTH), F32),
                        pltpu.VMEM((R + CONV_HALO, SSM_BC_WIDTH), F32),
                        pltpu.VMEM((SSM_STATE, SSM_WIDTH), F32),
                        pltpu.VMEM((R, SSM_WIDTH), F32)],
        compiler_params=_params("parallel", "arbitrary"),
        name="ssd_mixer",
    )(*args)


def _outproj_kernel(attn_ref, ssm_ref, w_ref, x_ref, gpost_ref, gnext_ref, xo_ref, ho_ref, wb_ref, *, sub):
    @pl.when(pl.program_id(0) == 0)
    def _():
        wb_ref[...] = w_ref[...].astype(BF16)

    for r0 in range(0, x_ref.shape[0], sub):
        rows = slice(r0, r0 + sub)
        mixed = _dot(jnp.concatenate([attn_ref[rows, :], ssm_ref[rows, :]], axis=1), wb_ref[...])
        x_new = x_ref[rows, :] + _rms(mixed, gpost_ref[...])
        xo_ref[rows, :] = x_new
        ho_ref[rows, :] = _rms(x_new, gnext_ref[...]).astype(ho_ref.dtype)


def _outproj(attn, ssm, w, layer, x, g_post, g_next, tm=512, sub=256):
    t, d = x.shape
    row = pl.BlockSpec((1, d), lambda m: (0, 0))
    return pl.pallas_call(
        functools.partial(_outproj_kernel, sub=sub),
        grid=(t // tm,),
        in_specs=[pl.BlockSpec((tm, ATTN_WIDTH), lambda m: (m, 0)),
                  pl.BlockSpec((tm, SSM_WIDTH), lambda m: (m, 0)),
                  pl.BlockSpec((None,) + w.shape[1:], lambda m: (layer, 0, 0), pipeline_mode=pl.Buffered(1)),
                  pl.BlockSpec((tm, d), lambda m: (m, 0)),
                  row, row],
        out_specs=[pl.BlockSpec((tm, d), lambda m: (m, 0)), pl.BlockSpec((tm, d), lambda m: (m, 0))],
        out_shape=[jax.ShapeDtypeStruct((t, d), F32), jax.ShapeDtypeStruct((t, d), BF16)],
        scratch_shapes=[pltpu.VMEM(w.shape[1:], BF16)],
        compiler_params=_params("arbitrary"),
        name="out_proj",
    )(attn, ssm, w, x, g_post.reshape(1, d), g_next.reshape(1, d))


def _ffn_up_kernel(h_ref, wg_ref, wu_ref, cwg_ref, cwu_ref, cbg_ref, cbu_ref, o_ref, wb_ref, *, row_tiles):
    assert sum(row_tiles) == h_ref.shape[0]
    H = CONV_HALO
    tn = o_ref.shape[1]

    @pl.when(pl.program_id(1) == 0)
    def _():
        wb_ref[:, 0:tn] = wg_ref[...].astype(BF16)
        wb_ref[:, tn:] = wu_ref[...].astype(BF16)

    def conv(u, cw_ref, cb_ref):
        acc = cw_ref[0:1, :] * u
        for k in range(1, FFN_CONV):
            acc = pltpu.roll(acc, 1, 0) + cw_ref[k:k + 1, :] * u
        return acc[H:, :] + cb_ref[...]

    r0 = 0
    tail = jnp.zeros((H, 2 * tn), F32)
    for tm in row_tiles:
        res = _dot(h_ref[r0:r0 + tm, :], wb_ref[...])
        u = jnp.concatenate([tail, res], axis=0)
        gate = conv(u[:, 0:tn], cwg_ref, cbg_ref)
        up = conv(u[:, tn:], cwu_ref, cbu_ref)
        o_ref[r0:r0 + tm, :] = (jax.nn.gelu(gate, approximate=True) * up).astype(o_ref.dtype)
        tail = res[tm - H:, :]
        r0 += tm


def _ffn_up(h, w_up, conv_w, conv_b, layer, seq, row_tiles=(1024, 1024), tn=512):
    t, d = h.shape
    nt = FFN_HIDDEN // tn
    col = lambda rows, half: pl.BlockSpec((None, rows, tn), lambda j, b: (layer, 0, half * nt + j))
    return pl.pallas_call(
        functools.partial(_ffn_up_kernel, row_tiles=row_tiles),
        grid=(nt, t // seq),
        in_specs=[pl.BlockSpec((seq, d), lambda j, b: (b, 0)),
                  col(d, 0), col(d, 1), col(FFN_CONV, 0), col(FFN_CONV, 1), col(1, 0), col(1, 1)],
        out_specs=pl.BlockSpec((seq, tn), lambda j, b: (b, j)),
        out_shape=jax.ShapeDtypeStruct((t, FFN_HIDDEN), BF16),
        scratch_shapes=[pltpu.VMEM((d, 2 * tn), BF16)],
        compiler_params=_params("parallel", "arbitrary"),
        name="ffn_up_conv_geglu",
    )(h, w_up, w_up, conv_w, conv_w, conv_b, conv_b)


def _ffn_down_kernel(a_ref, w_ref, x_ref, gpost_ref, gnext_ref, xo_ref, ho_ref, *, sub):
    for r0 in range(0, x_ref.shape[0], sub):
        rows = slice(r0, r0 + sub)
        x_new = x_ref[rows, :] + _rms(_dot(a_ref[rows, :], w_ref[...]), gpost_ref[...])
        xo_ref[rows, :] = x_new
        ho_ref[rows, :] = _rms(x_new, gnext_ref[...]).astype(ho_ref.dtype)


def _ffn_down(act, w, x, g_post, g_next, tm=512, sub=256):
    t, d = x.shape
    kdim = act.shape[1]
    row = pl.BlockSpec((1, d), lambda m: (0, 0))
    return pl.pallas_call(
        functools.partial(_ffn_down_kernel, sub=sub),
        grid=(t // tm,),
        in_specs=[pl.BlockSpec((tm, kdim), lambda m: (m, 0)),
                  pl.BlockSpec((kdim, d), lambda m: (0, 0), pipeline_mode=pl.Buffered(1)),
                  pl.BlockSpec((tm, d), lambda m: (m, 0)),
                  row, row],
        out_specs=[pl.BlockSpec((tm, d), lambda m: (m, 0)), pl.BlockSpec((tm, d), lambda m: (m, 0))],
        out_shape=[jax.ShapeDtypeStruct((t, d), F32), jax.ShapeDtypeStruct((t, d), BF16)],
        compiler_params=_params("parallel"),
        name="ffn_down",
    )(act, w, x, g_post.reshape(1, d), g_next.reshape(1, d))


def kernel(x, rel_bias, ln_mix_pre, w_in, ssm_conv_w, ssm_conv_b, dt_bias, a_log, d_skip, ssm_norm_w, w_out,
           ln_mix_post, ln_ffn_pre, w_ffn_up, ffn_conv_w, ffn_conv_b, w_ffn_down, ln_ffn_post):
    bsz, seq, d = x.shape
    depth = w_in.shape[0]
    t = bsz * seq
    assert w_in.shape[2] == QKV_COLS + REST_COLS + SSM_HEADS

    rel_bias_t = rel_bias.T.astype(F32)
    town, tprev = _bias_tiles(rel_bias_t)
    w_in_t = jnp.swapaxes(w_in, 1, 2)
    w_dt_t = jnp.pad(w_in_t[:, QKV_COLS + REST_COLS:, :], ((0, 0), (0, LANES - SSM_HEADS), (0, 0)))
    ffn_conv_b3 = ffn_conv_b.reshape(depth, 1, -1)

    xf = x.reshape(t, d)
    h = _rmsnorm(xf, ln_mix_pre[0])
    for l in range(depth):
        qkv, rest, dt_raw, w_down_bf16 = _inproj(h, w_in_t, w_dt_t, w_ffn_down, l)
        attn = _moba(qkv.reshape(bsz, seq, QKV_COLS), rel_bias_t, town, tprev, bsz, seq)
        ssm = _ssd(rest.reshape(bsz, seq, REST_COLS), dt_raw.reshape(bsz, seq, LANES), ssm_conv_w[l],
                   ssm_conv_b[l], dt_bias[l], a_log[l], d_skip[l], ssm_norm_w[l], bsz, seq)
        xf, h = _outproj(attn.reshape(t, ATTN_WIDTH), ssm.reshape(t, SSM_WIDTH), w_out, l, xf,
                         ln_mix_post[l], ln_ffn_pre[l])
        act = _ffn_up(h, w_ffn_up, ffn_conv_w, ffn_conv_b3, l, seq)
        g_next = ln_mix_pre[(l + 1) % depth]
        xf, h = _ffn_down(act, w_down_bf16, xf, ln_ffn_post[l], g_next)
    return xf.reshape(bsz, seq, d)
```
